```python
import math
import jax, jax.numpy as jnp
from jax import lax
import numpy as np

D_MODEL = 1024
BATCH = 8
SEQ = 4096
DEPTH = 2

EPS = 1e-6
ROPE_THETA = 500000.0
Q_BLOCK = 128

MLA_HEADS = 8
MLA_NOPE = 64
MLA_ROPE = 32
MLA_V = 64
Q_LORA = 384
KV_LORA = 256
MLA_WIDTH = MLA_HEADS * MLA_V

POOL_WINDOWS = (2, 4, 8, 16)
POOL_GROUP = 128
POOL_WIDTH = POOL_GROUP * len(POOL_WINDOWS)

IN_A = Q_LORA + KV_LORA + MLA_ROPE + POOL_WIDTH

DIFF_HEADS = 8
DIFF_HD = 64
DIFF_V = 2 * DIFF_HD
ROT_DIFF = DIFF_HD // 4
DIFF_QK_WIDTH = DIFF_HEADS * 2 * DIFF_HD
DIFF_V_WIDTH = DIFF_HEADS * DIFF_V

D_FF = -(-8 * D_MODEL // (3 * 256)) * 256

N_EVEN = (DEPTH + 1) // 2
N_ODD = DEPTH // 2

kernel_name = "hybrid_mla_pool_diffattn_adaln"


def rms_norm(x, g):
    xf = x.astype(jnp.float32)
    y = xf * lax.rsqrt(jnp.mean(xf * xf, axis=-1, keepdims=True) + EPS)
    return (y * g.astype(jnp.float32)).astype(x.dtype)


def rope_tables(positions, dim):
    inv = ROPE_THETA ** (-jnp.arange(0, dim, 2, dtype=jnp.float32) / dim)
    ang = positions.astype(jnp.float32)[..., None] * inv
    return jnp.cos(ang), jnp.sin(ang)


def apply_rope(x, cos, sin):
    x1, x2 = jnp.split(x, 2, axis=-1)
    cos = cos.astype(x.dtype)
    sin = sin.astype(x.dtype)
    return jnp.concatenate([x1 * cos - x2 * sin, x2 * cos + x1 * sin], axis=-1)


def query_blocks(t):
    b, s = t.shape[:2]
    t = t.reshape((b, s // Q_BLOCK, Q_BLOCK) + t.shape[2:])
    return jnp.moveaxis(t, 1, 0)


def merge_blocks(t):
    t = jnp.moveaxis(t, 0, 1)
    return t.reshape((t.shape[0], t.shape[1] * t.shape[2]) + t.shape[3:])


def causal_mask(blk, s):
    qpos = blk * Q_BLOCK + jnp.arange(Q_BLOCK)
    return jnp.arange(s)[None, :] <= qpos[:, None]


def mla_attention(q_nope, q_rope, k_nope, k_rope, v):
    s = k_nope.shape[1]
    scale = (MLA_NOPE + MLA_ROPE) ** -0.5

    def one_block(args):
        qn, qr, blk = args
        sc = (jnp.einsum('bqhd,bkhd->bhqk', qn, k_nope, preferred_element_type=jnp.float32)
              + jnp.einsum('bqhr,bkr->bhqk', qr, k_rope, preferred_element_type=jnp.float32))
        sc = jnp.where(causal_mask(blk, s), sc * scale, -jnp.inf)
        p = jax.nn.softmax(sc, axis=-1).astype(v.dtype)
        return jnp.einsum('bhqk,bkhd->bqhd', p, v)

    out = lax.map(one_block, (query_blocks(q_nope), query_blocks(q_rope),
                              jnp.arange(s // Q_BLOCK)))
    return merge_blocks(out)


def pool_mixer(u, w, b, scale):
    bsz, s, _ = u.shape
    uf = u.astype(jnp.float32)
    cs = jnp.cumsum(uf, axis=1)
    count = jnp.arange(1, s + 1, dtype=jnp.float32)[None, :, None]
    outs = []
    for g, win in enumerate(POOL_WINDOWS):
        csg = cs[..., g * POOL_GROUP:(g + 1) * POOL_GROUP]
        prev = jnp.pad(csg, ((0, 0), (win, 0), (0, 0)))[:, :s]
        mean = (csg - prev) / jnp.minimum(count, win)
        outs.append(mean - uf[..., g * POOL_GROUP:(g + 1) * POOL_GROUP])
    pooled = jnp.stack(outs, axis=2).astype(u.dtype)
    y = jnp.einsum('bsgc,gcd->bsgd', pooled, w) + b
    return y.reshape(bsz, s, POOL_WIDTH) * scale


def mla_pool_mixer(h, cos_r, sin_r, w_in, q_norm_g, kv_norm_g, w_uq, w_ukv,
                   p_w, p_b, p_scale, w_out):
    bsz, s, _ = h.shape
    proj = h @ w_in
    c_q, c_kv, k_rope, u = jnp.split(
        proj, [Q_LORA, Q_LORA + KV_LORA, Q_LORA + KV_LORA + MLA_ROPE], axis=-1)
    q = (rms_norm(c_q, q_norm_g) @ w_uq).reshape(bsz, s, MLA_HEADS, MLA_NOPE + MLA_ROPE)
    q_nope, q_rope = q[..., :MLA_NOPE], q[..., MLA_NOPE:]
    kv = (rms_norm(c_kv, kv_norm_g) @ w_ukv).reshape(bsz, s, MLA_HEADS, MLA_NOPE + MLA_V)
    k_nope, v = kv[..., :MLA_NOPE], kv[..., MLA_NOPE:]
    q_rope = apply_rope(q_rope, cos_r[:, :, None], sin_r[:, :, None])
    k_rope = apply_rope(k_rope, cos_r, sin_r)
    attn = mla_attention(q_nope, q_rope, k_nope, k_rope, v).reshape(bsz, s, MLA_WIDTH)
    pool = pool_mixer(u, p_w, p_b, p_scale)
    return jnp.concatenate([attn, pool], axis=-1) @ w_out


def diff_attention_mixer(h, cos_p, sin_p, w_qkv, lq1, lk1, lq2, lk2, subln_g, w_out,
                         lambda_init):
    bsz, s, _ = h.shape
    proj = h @ w_qkv
    q, k, v = jnp.split(proj, [DIFF_QK_WIDTH, 2 * DIFF_QK_WIDTH], axis=-1)
    q = q.reshape(bsz, s, DIFF_HEADS, 2, DIFF_HD)
    k = k.reshape(bsz, s, DIFF_HEADS, 2, DIFF_HD)
    v = v.reshape(bsz, s, DIFF_HEADS, DIFF_V)
    cp, sp = cos_p[:, :, None, None], sin_p[:, :, None, None]
    q = jnp.concatenate([apply_rope(q[..., :ROT_DIFF], cp, sp), q[..., ROT_DIFF:]], axis=-1)
    k = jnp.concatenate([apply_rope(k[..., :ROT_DIFF], cp, sp), k[..., ROT_DIFF:]], axis=-1)
    lam = (jnp.exp(jnp.sum(lq1.astype(jnp.float32) * lk1.astype(jnp.float32)))
           - jnp.exp(jnp.sum(lq2.astype(jnp.float32) * lk2.astype(jnp.float32)))
           + lambda_init)
    scale = DIFF_HD ** -0.5

    def one_block(args):
        qb, blk = args
        sc = jnp.einsum('bqhcd,bkhcd->bhcqk', qb, k, preferred_element_type=jnp.float32)
        sc = jnp.where(causal_mask(blk, s), sc * scale, -jnp.inf)
        p = jax.nn.softmax(sc, axis=-1)
        a = (p[:, :, 0] - lam * p[:, :, 1]).astype(v.dtype)
        return jnp.einsum('bhqk,bkhd->bqhd', a, v)

    out = merge_blocks(lax.map(one_block, (query_blocks(q), jnp.arange(s // Q_BLOCK))))
    out = rms_norm(out, subln_g) * (1.0 - lambda_init)
    return out.reshape(bsz, s, DIFF_V_WIDTH) @ w_out


def swiglu(h, w_gate_up, w_down):
    g, u = jnp.split(h @ w_gate_up, 2, axis=-1)
    return (jax.nn.silu(g) * u) @ w_down


def diff_lambda_init(layer_idx):
    return 0.8 - 0.6 * math.exp(-0.3 * layer_idx)


def setup_inputs(seed: int = 0) -> dict:
    key = jax.random.key(seed)
    ks = iter(jax.random.split(key, 32))

    def nrm(shape, scale):
        return jax.random.normal(next(ks), shape, jnp.float32) * scale

    def gain(shape):
        return 1.0 + nrm(shape, 0.05)

    D = D_MODEL
    x = nrm((BATCH, SEQ, D), 1.0)
    c = nrm((BATCH, D), 1.0)
    offsets = jax.random.randint(next(ks), (BATCH, 1), 0, 1024, dtype=jnp.int32)
    positions = (jnp.arange(SEQ, dtype=jnp.int32)[None, :] + offsets).astype(jnp.int32)
    return {
        "x": x,
        "c": c,
        "positions": positions,
        "ada_w": nrm((DEPTH, D, 6 * D), 0.5 * D ** -0.5),
        "ada_b": nrm((DEPTH, 6 * D), 0.02),
        "norm1_g": gain((DEPTH, D)),
        "norm2_g": gain((DEPTH, D)),
        "ffn_w_gate_up": nrm((DEPTH, D, 2 * D_FF), D ** -0.5),
        "ffn_w_down": nrm((DEPTH, D_FF, D), D_FF ** -0.5),
        "mla_w_in": nrm((N_EVEN, D, IN_A), D ** -0.5),
        "mla_q_norm_g": gain((N_EVEN, Q_LORA)),
        "mla_kv_norm_g": gain((N_EVEN, KV_LORA)),
        "mla_w_uq": nrm((N_EVEN, Q_LORA, MLA_HEADS * (MLA_NOPE + MLA_ROPE)), Q_LORA ** -0.5),
        "mla_w_ukv": nrm((N_EVEN, KV_LORA, MLA_HEADS * (MLA_NOPE + MLA_V)), KV_LORA ** -0.5),
        "pool_w": nrm((N_EVEN, len(POOL_WINDOWS), POOL_GROUP, POOL_GROUP), POOL_GROUP ** -0.5),
        "pool_b": nrm((N_EVEN, len(POOL_WINDOWS), POOL_GROUP), 0.02),
        "pool_scale": gain((N_EVEN, POOL_WIDTH)),
        "mix_a_w_out": nrm((N_EVEN, MLA_WIDTH + POOL_WIDTH, D), (MLA_WIDTH + POOL_WIDTH) ** -0.5),
        "diff_w_qkv": nrm((N_ODD, D, 2 * DIFF_QK_WIDTH + DIFF_V_WIDTH), D ** -0.5),
        "diff_lambda_q1": nrm((N_ODD, DIFF_HD), 0.1),
        "diff_lambda_k1": nrm((N_ODD, DIFF_HD), 0.1),
        "diff_lambda_q2": nrm((N_ODD, DIFF_HD), 0.1),
        "diff_lambda_k2": nrm((N_ODD, DIFF_HD), 0.1),
        "diff_subln_g": gain((N_ODD, DIFF_V)),
        "diff_w_out": nrm((N_ODD, DIFF_V_WIDTH, D), DIFF_V_WIDTH ** -0.5),
        "final_norm_g": gain((D,)),
    }


def reference(x, c, positions, ada_w, ada_b, norm1_g, norm2_g, ffn_w_gate_up, ffn_w_down,
              mla_w_in, mla_q_norm_g, mla_kv_norm_g, mla_w_uq, mla_w_ukv, pool_w, pool_b,
              pool_scale, mix_a_w_out, diff_w_qkv, diff_lambda_q1, diff_lambda_k1,
              diff_lambda_q2, diff_lambda_k2, diff_subln_g, diff_w_out, final_norm_g):
    cos_r, sin_r = rope_tables(positions, MLA_ROPE)
    cos_p, sin_p = rope_tables(positions, ROT_DIFF)
    cond = jax.nn.silu(c)
    for i in range(DEPTH):
        mod = cond @ ada_w[i] + ada_b[i]
        sh1, sc1, g1, sh2, sc2, g2 = [m[:, None, :] for m in jnp.split(mod, 6, axis=-1)]
        h = rms_norm(x, norm1_g[i]) * (1.0 + sc1) + sh1
        j = i // 2
        if i % 2 == 0:
            y = mla_pool_mixer(h, cos_r, sin_r, mla_w_in[j], mla_q_norm_g[j], mla_kv_norm_g[j],
                               mla_w_uq[j], mla_w_ukv[j], pool_w[j], pool_b[j],
                               pool_scale[j], mix_a_w_out[j])
        else:
            y = diff_attention_mixer(h, cos_p, sin_p, diff_w_qkv[j], diff_lambda_q1[j],
                                     diff_lambda_k1[j], diff_lambda_q2[j], diff_lambda_k2[j],
                                     diff_subln_g[j], diff_w_out[j], diff_lambda_init(i))
        x = x + g1 * y
        h = rms_norm(x, norm2_g[i]) * (1.0 + sc2) + sh2
        x = x + g2 * swiglu(h, ffn_w_gate_up[i], ffn_w_down[i])
    return rms_norm(x, final_norm_g)
```

```python
import functools
import math

import jax
import jax.numpy as jnp
from jax import lax
from jax.experimental import pallas as pl
from jax.experimental.pallas import tpu as pltpu

D_MODEL = 1024
EPS = 1e-6
ROPE_THETA = 500000.0
MLA_HEADS = 8
MLA_NOPE = 64
MLA_ROPE = 32
MLA_V = 64
Q_LORA = 384
KV_LORA = 256
POOL_WINDOWS = (2, 4, 8, 16)
POOL_GROUP = 128
POOL_WIDTH = POOL_GROUP * len(POOL_WINDOWS)
DIFF_HEADS = 8
DIFF_HD = 64
ROT_DIFF = DIFF_HD // 4
D_FF = 2816

LANES = 128
POOL_HALO = 16
FF_CHUNK = 256
N_FF_CHUNKS = D_FF // FF_CHUNK
assert N_FF_CHUNKS * FF_CHUNK == D_FF

TOK_TILE = 512
MLA_TQ = 512
DIFF_TQ = 256
ATTN_TK = 512
VMEM_LIMIT = 56 * 1024 * 1024

LOG2E = 1.4426950408889634
MASK_VALUE = -1e30

F32 = jnp.float32
BF16 = jnp.bfloat16


def _const_spec(shape):
    nd = len(shape)
    return pl.BlockSpec(shape, lambda *_: (0,) * nd, pipeline_mode=pl.Buffered(1))


def _rms(x, g):
    return x * lax.rsqrt(jnp.mean(x * x, axis=-1, keepdims=True) + EPS) * g


def _rope_slab(x, cos_t, sin_t, is_x2, half):
    partner = jnp.where(is_x2, pltpu.roll(x, half, 1), pltpu.roll(x, LANES - half, 1))
    return x * cos_t + partner * sin_t


def _rope_table_kernel(pos_ref, inv_ref, cos_ref, sin_ref):
    ang = pos_ref[...].astype(F32) * inv_ref[...]
    cos_ref[...] = jnp.cos(ang)
    sin_ref[...] = jnp.sin(ang)


def _rope_tables(positions):
    b, s = positions.shape
    per_tok = 32
    toks_per_row = LANES // per_tok
    rows = b * s // toks_per_row
    inv_mla = ROPE_THETA ** (-jnp.arange(0, MLA_ROPE, 2, dtype=F32) / MLA_ROPE)
    inv_diff = ROPE_THETA ** (-jnp.arange(0, ROT_DIFF, 2, dtype=F32) / ROT_DIFF)
    inv = jnp.concatenate([inv_mla, inv_diff, jnp.zeros((per_tok - 24,), F32)])
    inv = jnp.tile(inv, toks_per_row).reshape(1, LANES)
    pos = jnp.repeat(positions.reshape(rows, toks_per_row), per_tok, axis=1)
    blk = 1024
    cos, sin = pl.pallas_call(
        _rope_table_kernel,
        grid=(rows // blk,),
        in_specs=[pl.BlockSpec((blk, LANES), lambda i: (i, 0)), _const_spec((1, LANES))],
        out_specs=[pl.BlockSpec((blk, LANES), lambda i: (i, 0))] * 2,
        out_shape=[jax.ShapeDtypeStruct((rows, LANES), F32)] * 2,
        name="rope_tables",
    )(pos, inv)
    cos = cos.reshape(b, s, per_tok)
    sin = sin.reshape(b, s, per_tok)
    c16, s16 = cos[..., :16], sin[..., :16]
    c8, s8 = cos[..., 16:24], sin[..., 16:24]
    one = lambda n: jnp.ones((b, s, n), F32)
    zero = lambda n: jnp.zeros((b, s, n), F32)
    cos_mla = jnp.concatenate([one(64), c16, c16, one(32)], axis=-1)
    sin_mla = jnp.concatenate([zero(64), -s16, s16, zero(32)], axis=-1)
    cos_diff = jnp.concatenate([c8, c8, one(48), c8, c8, one(48)], axis=-1)
    sin_diff = jnp.concatenate([-s8, s8, zero(48), -s8, s8, zero(48)], axis=-1)
    return cos_mla, sin_mla, cos_diff, sin_diff


def _mod_kernel(c_ref, w_ref, b_ref, o_ref):
    c = c_ref[...]
    cond = (c * jax.nn.sigmoid(c)).astype(BF16)
    o_ref[0] = jnp.dot(cond, w_ref[0].astype(BF16), preferred_element_type=F32) + b_ref[0]


def _modulation(c, ada_w, ada_b):
    depth, d, n = ada_w.shape
    b = c.shape[0]
    blk = 1536
    mod = pl.pallas_call(
        _mod_kernel,
        grid=(depth, n // blk),
        in_specs=[
            _const_spec((b, d)),
            pl.BlockSpec((1, d, blk), lambda i, j: (i, 0, j)),
            pl.BlockSpec((1, 1, blk), lambda i, j: (i, 0, j)),
        ],
        out_specs=pl.BlockSpec((1, b, blk), lambda i, j: (i, 0, j)),
        out_shape=jax.ShapeDtypeStruct((depth, b, n), F32),
        compiler_params=pltpu.CompilerParams(vmem_limit_bytes=VMEM_LIMIT),
        name="adaln_mod",
    )(c, ada_w, ada_b.reshape(depth, 1, n))
    return mod.reshape(depth, b, 6, d)


def _l0_pre_kernel(x_ref, mod_ref, n1g_ref, win_ref, qg_ref, kvg_ref, wq_ref, wkv_ref, cos_ref, sin_ref,
                   pw_ref, pb_ref, ps_ref, q_out, k_out, v_out, pool_out, halo_sc, *, q_scale):
    si = pl.program_id(1)
    t = x_ref.shape[1]

    @pl.when(si == 0)
    def _():
        halo_sc[...] = jnp.zeros_like(halo_sc)

    mod = mod_ref[0]
    h = _rms(x_ref[0], n1g_ref[...]) * (1.0 + mod[1:2]) + mod[0:1]
    proj = jnp.dot(h.astype(BF16), win_ref[...], preferred_element_type=F32)
    c_q = proj[:, :Q_LORA]
    c_kv = proj[:, Q_LORA:Q_LORA + KV_LORA]
    k_rope = proj[:, Q_LORA + KV_LORA:Q_LORA + KV_LORA + LANES]
    u = proj[:, Q_LORA + KV_LORA + LANES:]

    cos_t = cos_ref[0]
    sin_t = sin_ref[0]
    lane = lax.broadcasted_iota(jnp.int32, (t, LANES), 1)
    half = MLA_ROPE // 2
    is_x2 = (lane >= MLA_NOPE + half) & (lane < MLA_NOPE + MLA_ROPE)

    q = jnp.dot(_rms(c_q, qg_ref[...]).astype(BF16), wq_ref[...], preferred_element_type=F32)
    for hd in range(MLA_HEADS):
        sl = slice(hd * LANES, (hd + 1) * LANES)
        q_out[0, :, sl] = (_rope_slab(q[:, sl], cos_t, sin_t, is_x2, half) * q_scale).astype(BF16)

    kv = jnp.dot(_rms(c_kv, kvg_ref[...]).astype(BF16), wkv_ref[...], preferred_element_type=F32)
    k_rope = _rope_slab(k_rope, cos_t, sin_t, is_x2, half)
    for hd in range(MLA_HEADS):
        sl = slice(hd * LANES, (hd + 1) * LANES)
        k_out[0, :, sl] = (kv[:, sl] + k_rope).astype(BF16)
    v_out[0] = kv[:, MLA_HEADS * LANES:].astype(BF16)

    ext = jnp.concatenate([halo_sc[...], u], axis=0)
    halo_sc[...] = u[t - POOL_HALO:, :]
    tok = si * t + lax.broadcasted_iota(jnp.int32, (t, 1), 0)
    for g, win in enumerate(POOL_WINDOWS):
        sl = slice(g * POOL_GROUP, (g + 1) * POOL_GROUP)
        e = ext[:, sl]
        shift = 1
        while shift < win:
            e = e + pltpu.roll(e, shift, 0)
            shift *= 2
        cnt = jnp.minimum(tok + 1, win).astype(F32)
        pooled = e[POOL_HALO:] * (1.0 / cnt) - u[:, sl]
        y = jnp.dot(pooled.astype(BF16), pw_ref[g], preferred_element_type=F32) + pb_ref[:, sl]
        pool_out[0, :, sl] = (y * ps_ref[:, sl]).astype(BF16)


def _l0_pre(x, mod, n1g, w_in_ext, qg, kvg, wq, wkv, cos_t, sin_t, pw, pb, ps):
    b, s, d = x.shape
    t = TOK_TILE
    tok = lambda w: pl.BlockSpec((1, t, w), lambda i, j: (i, j, 0))
    q_scale = (MLA_NOPE + MLA_ROPE) ** -0.5 * LOG2E
    return pl.pallas_call(
        functools.partial(_l0_pre_kernel, q_scale=q_scale),
        grid=(b, s // t),
        in_specs=[
            tok(d),
            pl.BlockSpec((1, 6, d), lambda i, j: (i, 0, 0)),
            _const_spec(n1g.shape), _const_spec(w_in_ext.shape), _const_spec(qg.shape), _const_spec(kvg.shape),
            _const_spec(wq.shape), _const_spec(wkv.shape),
            tok(LANES), tok(LANES),
            _const_spec(pw.shape), _const_spec(pb.shape), _const_spec(ps.shape),
        ],
        out_specs=[tok(MLA_HEADS * LANES), tok(MLA_HEADS * LANES), tok(MLA_HEADS * MLA_V), tok(POOL_WIDTH)],
        out_shape=[
            jax.ShapeDtypeStruct((b, s, MLA_HEADS * LANES), BF16),
            jax.ShapeDtypeStruct((b, s, MLA_HEADS * LANES), BF16),
            jax.ShapeDtypeStruct((b, s, MLA_HEADS * MLA_V), BF16),
            jax.ShapeDtypeStruct((b, s, POOL_WIDTH), BF16),
        ],
        scratch_shapes=[pltpu.VMEM((POOL_HALO, POOL_WIDTH), F32)],
        compiler_params=pltpu.CompilerParams(
            dimension_semantics=("arbitrary", "arbitrary"), vmem_limit_bytes=VMEM_LIMIT),
        name="l0_pre",
    )(x, mod, n1g, w_in_ext, qg, kvg, wq, wkv, cos_t, sin_t, pw, pb, ps)


def _l1_pre_kernel(x_ref, mod_ref, n1g_ref, wqkv_ref, cos_ref, sin_ref, q_out, k_out, v_out, *, q_scale):
    t = x_ref.shape[1]
    mod = mod_ref[0]
    h = (_rms(x_ref[0], n1g_ref[...]) * (1.0 + mod[1:2]) + mod[0:1]).astype(BF16)
    cos_t = cos_ref[0]
    sin_t = sin_ref[0]
    lane = lax.broadcasted_iota(jnp.int32, (t, LANES), 1)
    half = ROT_DIFF // 2
    is_x2 = ((lane % DIFF_HD) >= half) & ((lane % DIFF_HD) < ROT_DIFF)
    width = DIFF_HEADS * LANES
    q = jnp.dot(h, wqkv_ref[:, :width], preferred_element_type=F32)
    for hd in range(DIFF_HEADS):
        sl = slice(hd * LANES, (hd + 1) * LANES)
        q_out[0, :, sl] = (_rope_slab(q[:, sl], cos_t, sin_t, is_x2, half) * q_scale).astype(BF16)
    k = jnp.dot(h, wqkv_ref[:, width:2 * width], preferred_element_type=F32)
    for hd in range(DIFF_HEADS):
        sl = slice(hd * LANES, (hd + 1) * LANES)
        k_out[0, :, sl] = _rope_slab(k[:, sl], cos_t, sin_t, is_x2, half).astype(BF16)
    v_out[0] = jnp.dot(h, wqkv_ref[:, 2 * width:], preferred_element_type=F32).astype(BF16)


def _l1_pre(x, mod, n1g, wqkv, cos_t, sin_t):
    b, s, d = x.shape
    t = TOK_TILE
    tok = lambda w: pl.BlockSpec((1, t, w), lambda i, j: (i, j, 0))
    width = DIFF_HEADS * LANES
    return pl.pallas_call(
        functools.partial(_l1_pre_kernel, q_scale=DIFF_HD ** -0.5 * LOG2E),
        grid=(b, s // t),
        in_specs=[
            tok(d),
            pl.BlockSpec((1, 6, d), lambda i, j: (i, 0, 0)),
            _const_spec(n1g.shape), _const_spec(wqkv.shape),
            tok(LANES), tok(LANES),
        ],
        out_specs=[tok(width)] * 3,
        out_shape=[jax.ShapeDtypeStruct((b, s, width), BF16)] * 3,
        compiler_params=pltpu.CompilerParams(
            dimension_semantics=("arbitrary", "arbitrary"), vmem_limit_bytes=VMEM_LIMIT),
        name="l1_pre",
    )(x, mod, n1g, wqkv, cos_t, sin_t)


def _flash_head(q, k_at, v_at, n_full, row_pos, m_sc, l_sc, acc_sc):
    m_sc[...] = jnp.full_like(m_sc, MASK_VALUE)
    l_sc[...] = jnp.zeros_like(l_sc)
    acc_sc[...] = jnp.zeros_like(acc_sc)

    def step(j, masked):
        s = lax.dot_general(q, k_at(j), (((1,), (1,)), ((), ())), preferred_element_type=F32)
        if masked:
            col = j * ATTN_TK + lax.broadcasted_iota(jnp.int32, s.shape, 1)
            s = jnp.where(col <= row_pos, s, MASK_VALUE)
        m_prev = m_sc[...]
        m_new = jnp.maximum(m_prev, jnp.max(s, axis=1, keepdims=True))
        alpha = jnp.exp2(m_prev - m_new)
        p = jnp.exp2(s - m_new)
        l_sc[...] = alpha * l_sc[...] + jnp.sum(p, axis=1, keepdims=True)
        acc_sc[...] = alpha * acc_sc[...] + jnp.dot(p.astype(BF16), v_at(j), preferred_element_type=F32)
        m_sc[...] = m_new

    def body(j, carry):
        step(j, False)
        return carry

    lax.fori_loop(0, n_full, body, 0)
    step(n_full, True)


def _mla_attn_kernel(q_ref, k_ref, v_ref, o_ref, m_sc, l_sc, acc_sc):
    tq = q_ref.shape[1]
    qi = pl.program_id(1)
    n_full = (qi * tq) // ATTN_TK
    row_pos = qi * tq + lax.broadcasted_iota(jnp.int32, (tq, 1), 0)
    lane = lax.broadcasted_iota(jnp.int32, (tq, LANES), 1)
    for pair in range(MLA_HEADS // 2):
        halves = []
        for hd in (2 * pair, 2 * pair + 1):
            sl = slice(hd * LANES, (hd + 1) * LANES)
            vsl = slice(pair * LANES, (pair + 1) * LANES)
            k_at = lambda j, sl=sl: k_ref[0, pl.ds(pl.multiple_of(j * ATTN_TK, ATTN_TK), ATTN_TK), sl]
            v_at = lambda j, vsl=vsl: v_ref[0, pl.ds(pl.multiple_of(j * ATTN_TK, ATTN_TK), ATTN_TK), vsl]
            _flash_head(q_ref[0, :, sl], k_at, v_at, n_full, row_pos, m_sc, l_sc, acc_sc)
            halves.append(acc_sc[...] / l_sc[...])
        o_ref[0, :, pair * LANES:(pair + 1) * LANES] = jnp.where(lane < MLA_V, halves[0], halves[1]).astype(BF16)


def _mla_attention(q, k, v):
    b, s, _ = q.shape
    tq = MLA_TQ
    return pl.pallas_call(
        _mla_attn_kernel,
        grid=(b, s // tq),
        in_specs=[
            pl.BlockSpec((1, tq, q.shape[2]), lambda i, j: (i, j, 0)),
            pl.BlockSpec((1, s, k.shape[2]), lambda i, j: (i, 0, 0)),
            pl.BlockSpec((1, s, v.shape[2]), lambda i, j: (i, 0, 0)),
        ],
        out_specs=pl.BlockSpec((1, tq, v.shape[2]), lambda i, j: (i, j, 0)),
        out_shape=jax.ShapeDtypeStruct(v.shape, BF16),
        scratch_shapes=[pltpu.VMEM((tq, 1), F32), pltpu.VMEM((tq, 1), F32), pltpu.VMEM((tq, LANES), F32)],
        compiler_params=pltpu.CompilerParams(
            dimension_semantics=("arbitrary", "arbitrary"), vmem_limit_bytes=VMEM_LIMIT),
        name="mla_attention",
    )(q, k, v)


def _diff_attn_kernel(q_ref, k_ref, v_ref, lam_ref, g_ref, o_ref, m_sc, l_sc, acc_sc, *, lambda_init):
    tq = q_ref.shape[1]
    qi = pl.program_id(1)
    n_full = (qi * tq) // ATTN_TK
    row = lax.broadcasted_iota(jnp.int32, (2 * tq, 1), 0)
    row_pos = qi * tq + jnp.where(row >= tq, row - tq, row)
    lane = lax.broadcasted_iota(jnp.int32, (tq, LANES), 1)
    lv = lam_ref[...]
    lam = (jnp.exp(jnp.sum(lv[0:1] * lv[1:2], axis=1, keepdims=True))
           - jnp.exp(jnp.sum(lv[2:3] * lv[3:4], axis=1, keepdims=True)) + lambda_init)
    for hd in range(DIFF_HEADS):
        sl = slice(hd * LANES, (hd + 1) * LANES)
        qh = q_ref[0, :, sl]
        zero = jnp.zeros_like(qh)
        qq = jnp.concatenate([jnp.where(lane < DIFF_HD, qh, zero), jnp.where(lane >= DIFF_HD, qh, zero)], axis=0)
        k_at = lambda j, sl=sl: k_ref[0, pl.ds(pl.multiple_of(j * ATTN_TK, ATTN_TK), ATTN_TK), sl]
        v_at = lambda j, sl=sl: v_ref[0, pl.ds(pl.multiple_of(j * ATTN_TK, ATTN_TK), ATTN_TK), sl]
        _flash_head(qq, k_at, v_at, n_full, row_pos, m_sc, l_sc, acc_sc)
        o = acc_sc[...] / l_sc[...]
        o = o[:tq] - lam * o[tq:]
        o_ref[0, :, sl] = (_rms(o, g_ref[...]) * (1.0 - lambda_init)).astype(BF16)


def _diff_attention(q, k, v, lam_vecs, subln_g, lambda_init):
    b, s, w = q.shape
    tq = DIFF_TQ
    return pl.pallas_call(
        functools.partial(_diff_attn_kernel, lambda_init=lambda_init),
        grid=(b, s // tq),
        in_specs=[
            pl.BlockSpec((1, tq, w), lambda i, j: (i, j, 0)),
            pl.BlockSpec((1, s, w), lambda i, j: (i, 0, 0)),
            pl.BlockSpec((1, s, w), lambda i, j: (i, 0, 0)),
            _const_spec(lam_vecs.shape), _const_spec(subln_g.shape),
        ],
        out_specs=pl.BlockSpec((1, tq, w), lambda i, j: (i, j, 0)),
        out_shape=jax.ShapeDtypeStruct((b, s, w), BF16),
        scratch_shapes=[pltpu.VMEM((2 * tq, 1), F32), pltpu.VMEM((2 * tq, 1), F32),
                        pltpu.VMEM((2 * tq, LANES), F32)],
        compiler_params=pltpu.CompilerParams(
            dimension_semantics=("arbitrary", "arbitrary"), vmem_limit_bytes=VMEM_LIMIT),
        name="diff_attention",
    )(q, k, v, lam_vecs, subln_g)


def _post_kernel(*refs, n_mix, final_norm):
    mix_refs = refs[:n_mix]
    x_ref, mod_ref, wout_ref, n2g_ref, wgu_ref, wd_ref = refs[n_mix:n_mix + 6]
    rest = refs[n_mix + 6:]
    if final_norm:
        fng_ref, o_ref, acc_sc = rest
    else:
        o_ref, acc_sc = rest
    mod = mod_ref[0]
    y = None
    row = 0
    for r in mix_refs:
        w = r.shape[2]
        part = jnp.dot(r[0], wout_ref[row:row + w, :], preferred_element_type=F32)
        y = part if y is None else y + part
        row += w
    x1 = x_ref[0] + mod[2:3] * y
    h = (_rms(x1, n2g_ref[...]) * (1.0 + mod[4:5]) + mod[3:4]).astype(BF16)
    acc_sc[...] = jnp.zeros_like(acc_sc)

    def body(c, carry):
        gu = jnp.dot(h, wgu_ref[c], preferred_element_type=F32)
        g = gu[:, :FF_CHUNK]
        act = (g * jax.nn.sigmoid(g) * gu[:, FF_CHUNK:]).astype(BF16)
        acc_sc[...] += jnp.dot(act, wd_ref[c], preferred_element_type=F32)
        return carry

    lax.fori_loop(0, N_FF_CHUNKS, body, 0)
    x2 = x1 + mod[5:6] * acc_sc[...]
    if final_norm:
        x2 = _rms(x2, fng_ref[...])
    o_ref[0] = x2


def _post(mix_inputs, x, mod, wout, n2g, wgu, wd, final_g=None):
    b, s, d = x.shape
    t = TOK_TILE
    tok = lambda w: pl.BlockSpec((1, t, w), lambda i, j: (i, j, 0))
    final_norm = final_g is not None
    in_specs = [tok(m.shape[2]) for m in mix_inputs] + [
        tok(d),
        pl.BlockSpec((1, 6, d), lambda i, j: (i, 0, 0)),
        _const_spec(wout.shape), _const_spec(n2g.shape), _const_spec(wgu.shape), _const_spec(wd.shape),
    ]
    args = list(mix_inputs) + [x, mod, wout, n2g, wgu, wd]
    if final_norm:
        in_specs.append(_const_spec(final_g.shape))
        args.append(final_g)
    return pl.pallas_call(
        functools.partial(_post_kernel, n_mix=len(mix_inputs), final_norm=final_norm),
        grid=(b, s // t),
        in_specs=in_specs,
        out_specs=tok(d),
        out_shape=jax.ShapeDtypeStruct((b, s, d), F32),
        scratch_shapes=[pltpu.VMEM((t, d), F32)],
        compiler_params=pltpu.CompilerParams(
            dimension_semantics=("arbitrary", "arbitrary"), vmem_limit_bytes=VMEM_LIMIT),
        name="post_final" if final_norm else "post",
    )(*args)


def _ffn_weights(w_gate_up, w_down):
    d = w_gate_up.shape[0]
    g = w_gate_up[:, :D_FF].reshape(d, N_FF_CHUNKS, FF_CHUNK)
    u = w_gate_up[:, D_FF:].reshape(d, N_FF_CHUNKS, FF_CHUNK)
    wgu = jnp.concatenate([g, u], axis=2).transpose(1, 0, 2).astype(BF16)
    wd = w_down.reshape(N_FF_CHUNKS, FF_CHUNK, d).astype(BF16)
    return wgu, wd


def kernel(x, c, positions, ada_w, ada_b, norm1_g, norm2_g, ffn_w_gate_up, ffn_w_down, mla_w_in, mla_q_norm_g,
           mla_kv_norm_g, mla_w_uq, mla_w_ukv, pool_w, pool_b, pool_scale, mix_a_w_out, diff_w_qkv,
           diff_lambda_q1, diff_lambda_k1, diff_lambda_q2, diff_lambda_k2, diff_subln_g, diff_w_out,
           final_norm_g):
    d = D_MODEL
    cos_mla, sin_mla, cos_diff, sin_diff = _rope_tables(positions)
    mod = _modulation(c, ada_w, ada_b)

    w_in = mla_w_in[0]
    o_kv, o_kr, o_u = Q_LORA, Q_LORA + KV_LORA, Q_LORA + KV_LORA + MLA_ROPE
    w_kr = jnp.pad(w_in[:, o_kr:o_u], ((0, 0), (MLA_NOPE, LANES - MLA_NOPE - MLA_ROPE)))
    w_in_ext = jnp.concatenate([w_in[:, :o_kr], w_kr, w_in[:, o_u:]], axis=1).astype(BF16)
    qk_dim = MLA_NOPE + MLA_ROPE
    wq = jnp.pad(mla_w_uq[0].reshape(Q_LORA, MLA_HEADS, qk_dim), ((0, 0), (0, 0), (0, LANES - qk_dim)))
    wq = wq.reshape(Q_LORA, MLA_HEADS * LANES).astype(BF16)
    w_ukv = mla_w_ukv[0].reshape(KV_LORA, MLA_HEADS, MLA_NOPE + MLA_V)
    wk = jnp.pad(w_ukv[..., :MLA_NOPE], ((0, 0), (0, 0), (0, LANES - MLA_NOPE))).reshape(KV_LORA, MLA_HEADS * LANES)
    wv = w_ukv[..., MLA_NOPE:].reshape(KV_LORA, MLA_HEADS * MLA_V)
    wkv = jnp.concatenate([wk, wv], axis=1).astype(BF16)
    q0, k0, v0, pool = _l0_pre(
        x, mod[0], norm1_g[0:1], w_in_ext, mla_q_norm_g, mla_kv_norm_g, wq, wkv, cos_mla, sin_mla,
        pool_w[0].astype(BF16), pool_b[0].reshape(1, POOL_WIDTH), pool_scale)
    attn0 = _mla_attention(q0, k0, v0)
    wgu0, wd0 = _ffn_weights(ffn_w_gate_up[0], ffn_w_down[0])
    x = _post([attn0, pool], x, mod[0], mix_a_w_out[0].astype(BF16), norm2_g[0:1], wgu0, wd0)

    lambda_init = 0.8 - 0.6 * math.exp(-0.3 * 1)
    q1, k1, v1 = _l1_pre(x, mod[1], norm1_g[1:2], diff_w_qkv[0].astype(BF16), cos_diff, sin_diff)
    lam_vecs = jnp.concatenate([diff_lambda_q1, diff_lambda_k1, diff_lambda_q2, diff_lambda_k2], axis=0)
    attn1 = _diff_attention(q1, k1, v1, lam_vecs, diff_subln_g, lambda_init)
    wgu1, wd1 = _ffn_weights(ffn_w_gate_up[1], ffn_w_down[1])
    return _post([attn1], x, mod[1], diff_w_out[0].astype(BF16), norm2_g[1:2], wgu1, wd1,
                 final_g=final_norm_g.reshape(1, d))
```

```python
import functools
import math

import jax
import jax.numpy as jnp
from jax import lax
from jax.experimental import pallas as pl
from jax.experimental.pallas import tpu as pltpu

D_MODEL = 1024
EPS = 1e-6
ROPE_THETA = 500000.0
MLA_HEADS = 8
MLA_NOPE = 64
MLA_ROPE = 32
MLA_V = 64
Q_LORA = 384
KV_LORA = 256
POOL_WINDOWS = (2, 4, 8, 16)
POOL_GROUP = 128
POOL_WIDTH = POOL_GROUP * len(POOL_WINDOWS)
DIFF_HEADS = 8
DIFF_HD = 64
ROT_DIFF = DIFF_HD // 4
D_FF = 2816

LANES = 128
POOL_HALO = 16
FF_CHUNK = 256
N_FF_CHUNKS = D_FF // FF_CHUNK
assert N_FF_CHUNKS * FF_CHUNK == D_FF

TOK_TILE = 512
MLA_TQ = 512
DIFF_TQ = 256
ATTN_TK = 512
ATTN_GROUP = 8
VMEM_LIMIT = 56 * 1024 * 1024

LOG2E = 1.4426950408889634
MASK_VALUE = -1e30

F32 = jnp.float32
BF16 = jnp.bfloat16


def _const_spec(shape):
    nd = len(shape)
    return pl.BlockSpec(shape, lambda *_: (0,) * nd, pipeline_mode=pl.Buffered(1))


def _rms(x, g):
    return x * lax.rsqrt(jnp.mean(x * x, axis=-1, keepdims=True) + EPS) * g


def _rope_slab(x, cos_t, sin_t, is_x2, half):
    partner = jnp.where(is_x2, pltpu.roll(x, half, 1), pltpu.roll(x, LANES - half, 1))
    return x * cos_t + partner * sin_t


def _rope_table_kernel(pos_ref, inv_ref, cos_ref, sin_ref):
    ang = pos_ref[...].astype(F32) * inv_ref[...]
    cos_ref[...] = jnp.cos(ang)
    sin_ref[...] = jnp.sin(ang)


def _rope_tables(positions):
    b, s = positions.shape
    per_tok = 32
    toks_per_row = LANES // per_tok
    rows = b * s // toks_per_row
    inv_mla = ROPE_THETA ** (-jnp.arange(0, MLA_ROPE, 2, dtype=F32) / MLA_ROPE)
    inv_diff = ROPE_THETA ** (-jnp.arange(0, ROT_DIFF, 2, dtype=F32) / ROT_DIFF)
    inv = jnp.concatenate([inv_mla, inv_diff, jnp.zeros((per_tok - 24,), F32)])
    inv = jnp.tile(inv, toks_per_row).reshape(1, LANES)
    pos = jnp.repeat(positions.reshape(rows, toks_per_row), per_tok, axis=1)
    blk = 1024
    cos, sin = pl.pallas_call(
        _rope_table_kernel,
        grid=(rows // blk,),
        in_specs=[pl.BlockSpec((blk, LANES), lambda i: (i, 0)), _const_spec((1, LANES))],
        out_specs=[pl.BlockSpec((blk, LANES), lambda i: (i, 0))] * 2,
        out_shape=[jax.ShapeDtypeStruct((rows, LANES), F32)] * 2,
        name="rope_tables",
    )(pos, inv)
    cos = cos.reshape(b, s, per_tok)
    sin = sin.reshape(b, s, per_tok)
    c16, s16 = cos[..., :16], sin[..., :16]
    c8, s8 = cos[..., 16:24], sin[..., 16:24]
    one = lambda n: jnp.ones((b, s, n), F32)
    zero = lambda n: jnp.zeros((b, s, n), F32)
    cos_mla = jnp.concatenate([one(64), c16, c16, one(32)], axis=-1)
    sin_mla = jnp.concatenate([zero(64), -s16, s16, zero(32)], axis=-1)
    cos_diff = jnp.concatenate([c8, c8, one(48), c8, c8, one(48)], axis=-1)
    sin_diff = jnp.concatenate([-s8, s8, zero(48), -s8, s8, zero(48)], axis=-1)
    return cos_mla, sin_mla, cos_diff, sin_diff


def _mod_kernel(c_ref, w_ref, b_ref, o_ref):
    c = c_ref[...]
    cond = (c * jax.nn.sigmoid(c)).astype(BF16)
    o_ref[0] = jnp.dot(cond, w_ref[0].astype(BF16), preferred_element_type=F32) + b_ref[0]


def _modulation(c, ada_w, ada_b):
    depth, d, n = ada_w.shape
    b = c.shape[0]
    blk = 1536
    mod = pl.pallas_call(
        _mod_kernel,
        grid=(depth, n // blk),
        in_specs=[
            _const_spec((b, d)),
            pl.BlockSpec((1, d, blk), lambda i, j: (i, 0, j)),
            pl.BlockSpec((1, 1, blk), lambda i, j: (i, 0, j)),
        ],
        out_specs=pl.BlockSpec((1, b, blk), lambda i, j: (i, 0, j)),
        out_shape=jax.ShapeDtypeStruct((depth, b, n), F32),
        compiler_params=pltpu.CompilerParams(vmem_limit_bytes=VMEM_LIMIT),
        name="adaln_mod",
    )(c, ada_w, ada_b.reshape(depth, 1, n))
    return mod.reshape(depth, b, 6, d)


def _l0_pre_kernel(x_ref, mod_ref, n1g_ref, win_ref, qg_ref, kvg_ref, wq_ref, wkv_ref, cos_ref, sin_ref,
                   pw_ref, pb_ref, ps_ref, q_out, k_out, v_out, pool_out, halo_sc, *, q_scale):
    si = pl.program_id(1)
    t = x_ref.shape[1]

    @pl.when(si == 0)
    def _():
        halo_sc[...] = jnp.zeros_like(halo_sc)

    mod = mod_ref[0]
    h = _rms(x_ref[0], n1g_ref[...]) * (1.0 + mod[1:2]) + mod[0:1]
    proj = jnp.dot(h.astype(BF16), win_ref[...], preferred_element_type=F32)
    c_q = proj[:, :Q_LORA]
    c_kv = proj[:, Q_LORA:Q_LORA + KV_LORA]
    k_rope = proj[:, Q_LORA + KV_LORA:Q_LORA + KV_LORA + LANES]
    u = proj[:, Q_LORA + KV_LORA + LANES:]

    cos_t = cos_ref[0]
    sin_t = sin_ref[0]
    lane = lax.broadcasted_iota(jnp.int32, (t, LANES), 1)
    half = MLA_ROPE // 2
    is_x2 = (lane >= MLA_NOPE + half) & (lane < MLA_NOPE + MLA_ROPE)

    q = jnp.dot(_rms(c_q, qg_ref[...]).astype(BF16), wq_ref[...], preferred_element_type=F32)
    for hd in range(MLA_HEADS):
        sl = slice(hd * LANES, (hd + 1) * LANES)
        q_out[0, :, sl] = (_rope_slab(q[:, sl], cos_t, sin_t, is_x2, half) * q_scale).astype(BF16)

    kv = jnp.dot(_rms(c_kv, kvg_ref[...]).astype(BF16), wkv_ref[...], preferred_element_type=F32)
    k_rope = _rope_slab(k_rope, cos_t, sin_t, is_x2, half)
    for hd in range(MLA_HEADS):
        sl = slice(hd * LANES, (hd + 1) * LANES)
        k_out[0, :, sl] = (kv[:, sl] + k_rope).astype(BF16)
    v_out[0] = kv[:, MLA_HEADS * LANES:].astype(BF16)

    ext = jnp.concatenate([halo_sc[...], u], axis=0)
    halo_sc[...] = u[t - POOL_HALO:, :]
    tok = si * t + lax.broadcasted_iota(jnp.int32, (t, 1), 0)
    for g, win in enumerate(POOL_WINDOWS):
        sl = slice(g * POOL_GROUP, (g + 1) * POOL_GROUP)
        e = ext[:, sl]
        shift = 1
        while shift < win:
            e = e + pltpu.roll(e, shift, 0)
            shift *= 2
        cnt = jnp.minimum(tok + 1, win).astype(F32)
        pooled = e[POOL_HALO:] * (1.0 / cnt) - u[:, sl]
        y = jnp.dot(pooled.astype(BF16), pw_ref[g], preferred_element_type=F32) + pb_ref[:, sl]
        pool_out[0, :, sl] = (y * ps_ref[:, sl]).astype(BF16)


def _l0_pre(x, mod, n1g, w_in_ext, qg, kvg, wq, wkv, cos_t, sin_t, pw, pb, ps):
    b, s, d = x.shape
    t = TOK_TILE
    tok = lambda w: pl.BlockSpec((1, t, w), lambda i, j: (i, j, 0))
    q_scale = (MLA_NOPE + MLA_ROPE) ** -0.5 * LOG2E
    return pl.pallas_call(
        functools.partial(_l0_pre_kernel, q_scale=q_scale),
        grid=(b, s // t),
        in_specs=[
            tok(d),
            pl.BlockSpec((1, 6, d), lambda i, j: (i, 0, 0)),
            _const_spec(n1g.shape), _const_spec(w_in_ext.shape), _const_spec(qg.shape), _const_spec(kvg.shape),
            _const_spec(wq.shape), _const_spec(wkv.shape),
            tok(LANES), tok(LANES),
            _const_spec(pw.shape), _const_spec(pb.shape), _const_spec(ps.shape),
        ],
        out_specs=[tok(MLA_HEADS * LANES), tok(MLA_HEADS * LANES), tok(MLA_HEADS * MLA_V), tok(POOL_WIDTH)],
        out_shape=[
            jax.ShapeDtypeStruct((b, s, MLA_HEADS * LANES), BF16),
            jax.ShapeDtypeStruct((b, s, MLA_HEADS * LANES), BF16),
            jax.ShapeDtypeStruct((b, s, MLA_HEADS * MLA_V), BF16),
            jax.ShapeDtypeStruct((b, s, POOL_WIDTH), BF16),
        ],
        scratch_shapes=[pltpu.VMEM((POOL_HALO, POOL_WIDTH), F32)],
        compiler_params=pltpu.CompilerParams(
            dimension_semantics=("arbitrary", "arbitrary"), vmem_limit_bytes=VMEM_LIMIT),
        name="l0_pre",
    )(x, mod, n1g, w_in_ext, qg, kvg, wq, wkv, cos_t, sin_t, pw, pb, ps)


def _l1_pre_kernel(x_ref, mod_ref, n1g_ref, wqkv_ref, cos_ref, sin_ref, q_out, k_out, v_out, *, q_scale):
    t = x_ref.shape[1]
    mod = mod_ref[0]
    h = (_rms(x_ref[0], n1g_ref[...]) * (1.0 + mod[1:2]) + mod[0:1]).astype(BF16)
    cos_t = cos_ref[0]
    sin_t = sin_ref[0]
    lane = lax.broadcasted_iota(jnp.int32, (t, LANES), 1)
    half = ROT_DIFF // 2
    is_x2 = ((lane % DIFF_HD) >= half) & ((lane % DIFF_HD) < ROT_DIFF)
    width = DIFF_HEADS * LANES
    q = jnp.dot(h, wqkv_ref[:, :width], preferred_element_type=F32)
    for hd in range(DIFF_HEADS):
        sl = slice(hd * LANES, (hd + 1) * LANES)
        q_out[0, :, sl] = (_rope_slab(q[:, sl], cos_t, sin_t, is_x2, half) * q_scale).astype(BF16)
    k = jnp.dot(h, wqkv_ref[:, width:2 * width], preferred_element_type=F32)
    for hd in range(DIFF_HEADS):
        sl = slice(hd * LANES, (hd + 1) * LANES)
        k_out[0, :, sl] = _rope_slab(k[:, sl], cos_t, sin_t, is_x2, half).astype(BF16)
    v_out[0] = jnp.dot(h, wqkv_ref[:, 2 * width:], preferred_element_type=F32).astype(BF16)


def _l1_pre(x, mod, n1g, wqkv, cos_t, sin_t):
    b, s, d = x.shape
    t = TOK_TILE
    tok = lambda w: pl.BlockSpec((1, t, w), lambda i, j: (i, j, 0))
    width = DIFF_HEADS * LANES
    return pl.pallas_call(
        functools.partial(_l1_pre_kernel, q_scale=DIFF_HD ** -0.5 * LOG2E),
        grid=(b, s // t),
        in_specs=[
            tok(d),
            pl.BlockSpec((1, 6, d), lambda i, j: (i, 0, 0)),
            _const_spec(n1g.shape), _const_spec(wqkv.shape),
            tok(LANES), tok(LANES),
        ],
        out_specs=[tok(width)] * 3,
        out_shape=[jax.ShapeDtypeStruct((b, s, width), BF16)] * 3,
        compiler_params=pltpu.CompilerParams(
            dimension_semantics=("arbitrary", "arbitrary"), vmem_limit_bytes=VMEM_LIMIT),
        name="l1_pre",
    )(x, mod, n1g, wqkv, cos_t, sin_t)


def _flash_group(qs, k_ats, v_ats, n_full, row_pos, m_sc, acc_sc):
    n_grp = len(qs)
    m_sc[...] = jnp.full_like(m_sc, MASK_VALUE)
    acc_sc[...] = jnp.zeros_like(acc_sc)
    ones = jnp.ones((ATTN_TK, LANES), BF16)
    n_blk = ATTN_TK // LANES

    def step(j, masked):
        for g in range(n_grp):
            s = lax.dot_general(qs[g], k_ats[g](j), (((1,), (1,)), ((), ())), preferred_element_type=F32)
            if masked:
                col = j * ATTN_TK + lax.broadcasted_iota(jnp.int32, s.shape, 1)
                s = jnp.where(col <= row_pos, s, MASK_VALUE)
            blocks = [s[:, c * LANES:(c + 1) * LANES] for c in range(n_blk)]
            m_blk = blocks[0]
            for blk in blocks[1:]:
                m_blk = jnp.maximum(m_blk, blk)
            m_prev = m_sc[g]
            m_new = jnp.maximum(m_prev, jnp.max(m_blk, axis=1, keepdims=True))
            alpha = jnp.exp2(m_prev - m_new)
            p = jnp.concatenate([jnp.exp2(blk - m_new).astype(BF16) for blk in blocks], axis=1)
            v_ext = jnp.concatenate([v_ats[g](j), ones], axis=1)
            pv = jnp.dot(p, v_ext, preferred_element_type=F32)
            acc_sc[g] = jnp.concatenate([alpha, alpha], axis=1) * acc_sc[g] + pv
            m_sc[g] = m_new

    def body(j, carry):
        step(j, False)
        return carry

    lax.fori_loop(0, n_full, body, 0)
    step(n_full, True)


def _key_step(ref, lanes):
    return lambda j: ref[0, pl.ds(pl.multiple_of(j * ATTN_TK, ATTN_TK), ATTN_TK), lanes]


def _mla_attn_kernel(q_ref, k_ref, v_ref, o_ref, m_sc, acc_sc):
    tq = q_ref.shape[1]
    qi = pl.program_id(1)
    n_full = (qi * tq) // ATTN_TK
    row_pos = qi * tq + lax.broadcasted_iota(jnp.int32, (tq, 1), 0)
    lane = lax.broadcasted_iota(jnp.int32, (tq, LANES), 1)
    n_grp = m_sc.shape[0]
    for first in range(0, MLA_HEADS, n_grp):
        heads = range(first, first + n_grp)
        slabs = [slice(hd * LANES, (hd + 1) * LANES) for hd in heads]
        vslabs = [slice((hd // 2) * LANES, (hd // 2 + 1) * LANES) for hd in heads]
        _flash_group([q_ref[0, :, sl] for sl in slabs], [_key_step(k_ref, sl) for sl in slabs],
                     [_key_step(v_ref, sl) for sl in vslabs], n_full, row_pos, m_sc, acc_sc)
        for g in range(0, n_grp, 2):
            even = acc_sc[g, :, :LANES] / acc_sc[g, :, LANES:LANES + 1]
            odd = acc_sc[g + 1, :, :LANES] / acc_sc[g + 1, :, LANES:LANES + 1]
            pair = (first + g) // 2
            o_ref[0, :, pair * LANES:(pair + 1) * LANES] = jnp.where(lane < MLA_V, even, odd).astype(BF16)


def _mla_attention(q, k, v):
    b, s, _ = q.shape
    tq = MLA_TQ
    return pl.pallas_call(
        _mla_attn_kernel,
        grid=(b, s // tq),
        in_specs=[
            pl.BlockSpec((1, tq, q.shape[2]), lambda i, j: (i, j, 0)),
            pl.BlockSpec((1, s, k.shape[2]), lambda i, j: (i, 0, 0)),
            pl.BlockSpec((1, s, v.shape[2]), lambda i, j: (i, 0, 0)),
        ],
        out_specs=pl.BlockSpec((1, tq, v.shape[2]), lambda i, j: (i, j, 0)),
        out_shape=jax.ShapeDtypeStruct(v.shape, BF16),
        scratch_shapes=[pltpu.VMEM((ATTN_GROUP, tq, LANES), F32), pltpu.VMEM((ATTN_GROUP, tq, 2 * LANES), F32)],
        compiler_params=pltpu.CompilerParams(
            dimension_semantics=("arbitrary", "arbitrary"), vmem_limit_bytes=VMEM_LIMIT),
        name="mla_attention",
    )(q, k, v)


def _diff_attn_kernel(q_ref, k_ref, v_ref, lam_ref, g_ref, o_ref, m_sc, acc_sc, *, lambda_init):
    tq = q_ref.shape[1]
    qi = pl.program_id(1)
    n_full = (qi * tq) // ATTN_TK
    row = lax.broadcasted_iota(jnp.int32, (2 * tq, 1), 0)
    row_pos = qi * tq + jnp.where(row >= tq, row - tq, row)
    lane = lax.broadcasted_iota(jnp.int32, (tq, LANES), 1)
    lv = lam_ref[...]
    lam = (jnp.exp(jnp.sum(lv[0:1] * lv[1:2], axis=1, keepdims=True))
           - jnp.exp(jnp.sum(lv[2:3] * lv[3:4], axis=1, keepdims=True)) + lambda_init)
    n_grp = m_sc.shape[0]
    for first in range(0, DIFF_HEADS, n_grp):
        slabs = [slice(hd * LANES, (hd + 1) * LANES) for hd in range(first, first + n_grp)]
        qs = []
        for sl in slabs:
            qh = q_ref[0, :, sl]
            zero = jnp.zeros_like(qh)
            qs.append(jnp.concatenate(
                [jnp.where(lane < DIFF_HD, qh, zero), jnp.where(lane >= DIFF_HD, qh, zero)], axis=0))
        _flash_group(qs, [_key_step(k_ref, sl) for sl in slabs], [_key_step(v_ref, sl) for sl in slabs],
                     n_full, row_pos, m_sc, acc_sc)
        for g, sl in enumerate(slabs):
            o = acc_sc[g, :, :LANES] / acc_sc[g, :, LANES:LANES + 1]
            o = o[:tq] - lam * o[tq:]
            o_ref[0, :, sl] = (_rms(o, g_ref[...]) * (1.0 - lambda_init)).astype(BF16)


def _diff_attention(q, k, v, lam_vecs, subln_g, lambda_init):
    b, s, w = q.shape
    tq = DIFF_TQ
    return pl.pallas_call(
        functools.partial(_diff_attn_kernel, lambda_init=lambda_init),
        grid=(b, s // tq),
        in_specs=[
            pl.BlockSpec((1, tq, w), lambda i, j: (i, j, 0)),
            pl.BlockSpec((1, s, w), lambda i, j: (i, 0, 0)),
            pl.BlockSpec((1, s, w), lambda i, j: (i, 0, 0)),
            _const_spec(lam_vecs.shape), _const_spec(subln_g.shape),
        ],
        out_specs=pl.BlockSpec((1, tq, w), lambda i, j: (i, j, 0)),
        out_shape=jax.ShapeDtypeStruct((b, s, w), BF16),
        scratch_shapes=[pltpu.VMEM((ATTN_GROUP, 2 * tq, LANES), F32),
                        pltpu.VMEM((ATTN_GROUP, 2 * tq, 2 * LANES), F32)],
        compiler_params=pltpu.CompilerParams(
            dimension_semantics=("arbitrary", "arbitrary"), vmem_limit_bytes=VMEM_LIMIT),
        name="diff_attention",
    )(q, k, v, lam_vecs, subln_g)


def _post_kernel(*refs, n_mix, final_norm):
    mix_refs = refs[:n_mix]
    x_ref, mod_ref, wout_ref, n2g_ref, wgu_ref, wd_ref = refs[n_mix:n_mix + 6]
    rest = refs[n_mix + 6:]
    if final_norm:
        fng_ref, o_ref, acc_sc = rest
    else:
        o_ref, acc_sc = rest
    mod = mod_ref[0]
    y = None
    row = 0
    for r in mix_refs:
        w = r.shape[2]
        part = jnp.dot(r[0], wout_ref[row:row + w, :], preferred_element_type=F32)
        y = part if y is None else y + part
        row += w
    x1 = x_ref[0] + mod[2:3] * y
    h = (_rms(x1, n2g_ref[...]) * (1.0 + mod[4:5]) + mod[3:4]).astype(BF16)
    acc_sc[...] = jnp.zeros_like(acc_sc)

    def body(c, carry):
        gu = jnp.dot(h, wgu_ref[c], preferred_element_type=F32)
        g = gu[:, :FF_CHUNK]
        act = (g * jax.nn.sigmoid(g) * gu[:, FF_CHUNK:]).astype(BF16)
        acc_sc[...] += jnp.dot(act, wd_ref[c], preferred_element_type=F32)
        return carry

    lax.fori_loop(0, N_FF_CHUNKS, body, 0)
    x2 = x1 + mod[5:6] * acc_sc[...]
    if final_norm:
        x2 = _rms(x2, fng_ref[...])
    o_ref[0] = x2


def _post(mix_inputs, x, mod, wout, n2g, wgu, wd, final_g=None):
    b, s, d = x.shape
    t = TOK_TILE
    tok = lambda w: pl.BlockSpec((1, t, w), lambda i, j: (i, j, 0))
    final_norm = final_g is not None
    in_specs = [tok(m.shape[2]) for m in mix_inputs] + [
        tok(d),
        pl.BlockSpec((1, 6, d), lambda i, j: (i, 0, 0)),
        _const_spec(wout.shape), _const_spec(n2g.shape), _const_spec(wgu.shape), _const_spec(wd.shape),
    ]
    args = list(mix_inputs) + [x, mod, wout, n2g, wgu, wd]
    if final_norm:
        in_specs.append(_const_spec(final_g.shape))
        args.append(final_g)
    return pl.pallas_call(
        functools.partial(_post_kernel, n_mix=len(mix_inputs), final_norm=final_norm),
        grid=(b, s // t),
        in_specs=in_specs,
        out_specs=tok(d),
        out_shape=jax.ShapeDtypeStruct((b, s, d), F32),
        scratch_shapes=[pltpu.VMEM((t, d), F32)],
        compiler_params=pltpu.CompilerParams(
            dimension_semantics=("arbitrary", "arbitrary"), vmem_limit_bytes=VMEM_LIMIT),
        name="post_final" if final_norm else "post",
    )(*args)


def _ffn_weights(w_gate_up, w_down):
    d = w_gate_up.shape[0]
    g = w_gate_up[:, :D_FF].reshape(d, N_FF_CHUNKS, FF_CHUNK)
    u = w_gate_up[:, D_FF:].reshape(d, N_FF_CHUNKS, FF_CHUNK)
    wgu = jnp.concatenate([g, u], axis=2).transpose(1, 0, 2).astype(BF16)
    wd = w_down.reshape(N_FF_CHUNKS, FF_CHUNK, d).astype(BF16)
    return wgu, wd


def kernel(x, c, positions, ada_w, ada_b, norm1_g, norm2_g, ffn_w_gate_up, ffn_w_down, mla_w_in, mla_q_norm_g,
           mla_kv_norm_g, mla_w_uq, mla_w_ukv, pool_w, pool_b, pool_scale, mix_a_w_out, diff_w_qkv,
           diff_lambda_q1, diff_lambda_k1, diff_lambda_q2, diff_lambda_k2, diff_subln_g, diff_w_out,
           final_norm_g):
    d = D_MODEL
    cos_mla, sin_mla, cos_diff, sin_diff = _rope_tables(positions)
    mod = _modulation(c, ada_w, ada_b)

    w_in = mla_w_in[0]
    o_kv, o_kr, o_u = Q_LORA, Q_LORA + KV_LORA, Q_LORA + KV_LORA + MLA_ROPE
    w_kr = jnp.pad(w_in[:, o_kr:o_u], ((0, 0), (MLA_NOPE, LANES - MLA_NOPE - MLA_ROPE)))
    w_in_ext = jnp.concatenate([w_in[:, :o_kr], w_kr, w_in[:, o_u:]], axis=1).astype(BF16)
    qk_dim = MLA_NOPE + MLA_ROPE
    wq = jnp.pad(mla_w_uq[0].reshape(Q_LORA, MLA_HEADS, qk_dim), ((0, 0), (0, 0), (0, LANES - qk_dim)))
    wq = wq.reshape(Q_LORA, MLA_HEADS * LANES).astype(BF16)
    w_ukv = mla_w_ukv[0].reshape(KV_LORA, MLA_HEADS, MLA_NOPE + MLA_V)
    wk = jnp.pad(w_ukv[..., :MLA_NOPE], ((0, 0), (0, 0), (0, LANES - MLA_NOPE))).reshape(KV_LORA, MLA_HEADS * LANES)
    wv = w_ukv[..., MLA_NOPE:].reshape(KV_LORA, MLA_HEADS * MLA_V)
    wkv = jnp.concatenate([wk, wv], axis=1).astype(BF16)
    q0, k0, v0, pool = _l0_pre(
        x, mod[0], norm1_g[0:1], w_in_ext, mla_q_norm_g, mla_kv_norm_g, wq, wkv, cos_mla, sin_mla,
        pool_w[0].astype(BF16), pool_b[0].reshape(1, POOL_WIDTH), pool_scale)
    attn0 = _mla_attention(q0, k0, v0)
    wgu0, wd0 = _ffn_weights(ffn_w_gate_up[0], ffn_w_down[0])
    x = _post([attn0, pool], x, mod[0], mix_a_w_out[0].astype(BF16), norm2_g[0:1], wgu0, wd0)

    lambda_init = 0.8 - 0.6 * math.exp(-0.3 * 1)
    q1, k1, v1 = _l1_pre(x, mod[1], norm1_g[1:2], diff_w_qkv[0].astype(BF16), cos_diff, sin_diff)
    lam_vecs = jnp.concatenate([diff_lambda_q1, diff_lambda_k1, diff_lambda_q2, diff_lambda_k2], axis=0)
    attn1 = _diff_attention(q1, k1, v1, lam_vecs, diff_subln_g, lambda_init)
    wgu1, wd1 = _ffn_weights(ffn_w_gate_up[1], ffn_w_down[1])
    return _post([attn1], x, mod[1], diff_w_out[0].astype(BF16), norm2_g[1:2], wgu1, wd1,
                 final_g=final_norm_g.reshape(1, d))
```

```python
import functools
import math

import jax
import jax.numpy as jnp
from jax import lax
from jax.experimental import pallas as pl
from jax.experimental.pallas import tpu as pltpu

D_MODEL = 1024
EPS = 1e-6
ROPE_THETA = 500000.0
MLA_HEADS = 8
MLA_NOPE = 64
MLA_ROPE = 32
MLA_V = 64
Q_LORA = 384
KV_LORA = 256
POOL_WINDOWS = (2, 4, 8, 16)
POOL_GROUP = 128
POOL_WIDTH = POOL_GROUP * len(POOL_WINDOWS)
DIFF_HEADS = 8
DIFF_HD = 64
ROT_DIFF = DIFF_HD // 4
D_FF = 2816

LANES = 128
POOL_HALO = 16
FF_CHUNK = 256
N_FF_CHUNKS = D_FF // FF_CHUNK
assert N_FF_CHUNKS * FF_CHUNK == D_FF

TOK_TILE = 512
POST_TILE = 1024
MLA_TQ = 512
DIFF_TQ = 256
ATTN_TK = 512
ATTN_GROUP = 8
VMEM_LIMIT = 56 * 1024 * 1024

LOG2E = 1.4426950408889634
MASK_VALUE = -1e30

F32 = jnp.float32
BF16 = jnp.bfloat16


def _const_spec(shape):
    nd = len(shape)
    return pl.BlockSpec(shape, lambda *_: (0,) * nd, pipeline_mode=pl.Buffered(1))


def _rms(x, g):
    return x * lax.rsqrt(jnp.mean(x * x, axis=-1, keepdims=True) + EPS) * g


def _rope_slab(x, cos_t, sin_t, is_x2, half):
    partner = jnp.where(is_x2, pltpu.roll(x, half, 1), pltpu.roll(x, LANES - half, 1))
    return x * cos_t + partner * sin_t


TAB_FREQS = 32
TAB_COS_MLA, TAB_COS_DIFF = 0, 16
TAB_SIN_MLA, TAB_SIN_DIFF = TAB_FREQS, TAB_FREQS + 16
TAB_SLOTS = LANES // TAB_FREQS


def _rope_table_kernel(pos_ref, inv_ref, tab_ref):
    ang = pos_ref[...].astype(F32) * inv_ref[...]
    cos = jnp.cos(ang)
    sin = jnp.sin(ang)
    lane = lax.broadcasted_iota(jnp.int32, ang.shape, 1)
    for slot in range(TAB_SLOTS):
        shift_c = (LANES - slot * TAB_FREQS) % LANES
        shift_s = (LANES + TAB_FREQS - slot * TAB_FREQS) % LANES
        c = cos if shift_c == 0 else pltpu.roll(cos, shift_c, 1)
        s = sin if shift_s == 0 else pltpu.roll(sin, shift_s, 1)
        tab_ref[slot] = jnp.where(lane < TAB_FREQS, c, jnp.where(lane < 2 * TAB_FREQS, s, 0.0))


def _rope_table(positions):
    b, s = positions.shape
    rows = b * s // TAB_SLOTS
    inv_mla = ROPE_THETA ** (-jnp.arange(0, MLA_ROPE, 2, dtype=F32) / MLA_ROPE)
    inv_diff = ROPE_THETA ** (-jnp.arange(0, ROT_DIFF, 2, dtype=F32) / ROT_DIFF)
    inv = jnp.concatenate([inv_mla, inv_diff, jnp.zeros((TAB_FREQS - 24,), F32)])
    inv = jnp.tile(inv, TAB_SLOTS).reshape(1, LANES)
    pos = jnp.broadcast_to(positions.reshape(TAB_SLOTS, rows).T[:, :, None], (rows, TAB_SLOTS, TAB_FREQS))
    pos = pos.reshape(rows, LANES)
    blk = 1024
    tab = pl.pallas_call(
        _rope_table_kernel,
        grid=(rows // blk,),
        in_specs=[pl.BlockSpec((blk, LANES), lambda i: (i, 0)), _const_spec((1, LANES))],
        out_specs=pl.BlockSpec((TAB_SLOTS, blk, LANES), lambda i: (0, i, 0)),
        out_shape=jax.ShapeDtypeStruct((TAB_SLOTS, rows, LANES), F32),
        name="rope_table",
    )(pos, inv)
    return tab.reshape(b, s, LANES)


def _lane_pattern(tab, lane, pieces, default):
    out = jnp.full_like(tab, default)
    for lo, hi, src, sign in pieces:
        rolled = pltpu.roll(tab, (lo - src) % LANES, 1)
        out = jnp.where((lane >= lo) & (lane < hi), -rolled if sign < 0 else rolled, out)
    return out


def _rope_patterns(tab, lane, cos_src, sin_src, groups, half):
    cos_pieces, sin_pieces = [], []
    for g in groups:
        cos_pieces += [(g, g + half, cos_src, 1), (g + half, g + 2 * half, cos_src, 1)]
        sin_pieces += [(g, g + half, sin_src, -1), (g + half, g + 2 * half, sin_src, 1)]
    return _lane_pattern(tab, lane, cos_pieces, 1.0), _lane_pattern(tab, lane, sin_pieces, 0.0)


def _mod_kernel(c_ref, w_ref, b_ref, o_ref):
    c = c_ref[...]
    cond = (c * jax.nn.sigmoid(c)).astype(BF16)
    o_ref[0] = jnp.dot(cond, w_ref[0].astype(BF16), preferred_element_type=F32) + b_ref[0]


def _modulation(c, ada_w, ada_b):
    depth, d, n = ada_w.shape
    b = c.shape[0]
    blk = 1536
    mod = pl.pallas_call(
        _mod_kernel,
        grid=(depth, n // blk),
        in_specs=[
            _const_spec((b, d)),
            pl.BlockSpec((1, d, blk), lambda i, j: (i, 0, j)),
            pl.BlockSpec((1, 1, blk), lambda i, j: (i, 0, j)),
        ],
        out_specs=pl.BlockSpec((1, b, blk), lambda i, j: (i, 0, j)),
        out_shape=jax.ShapeDtypeStruct((depth, b, n), F32),
        compiler_params=pltpu.CompilerParams(vmem_limit_bytes=VMEM_LIMIT),
        name="adaln_mod",
    )(c, ada_w, ada_b.reshape(depth, 1, n))
    return mod.reshape(depth, b, 6, d)


def _l0_pre_kernel(x_ref, mod_ref, n1g_ref, win_ref, qg_ref, kvg_ref, wq_ref, wkv_ref, tab_ref,
                   pw_ref, pb_ref, ps_ref, q_out, k_out, v_out, pool_out, halo_sc, *, q_scale):
    si = pl.program_id(1)
    t = x_ref.shape[1]

    @pl.when(si == 0)
    def _():
        halo_sc[...] = jnp.zeros_like(halo_sc)

    mod = mod_ref[0]
    h = _rms(x_ref[0], n1g_ref[...]) * (1.0 + mod[1:2]) + mod[0:1]
    proj = jnp.dot(h.astype(BF16), win_ref[...], preferred_element_type=F32)
    c_q = proj[:, :Q_LORA]
    c_kv = proj[:, Q_LORA:Q_LORA + KV_LORA]
    k_rope = proj[:, Q_LORA + KV_LORA:Q_LORA + KV_LORA + LANES]
    u = proj[:, Q_LORA + KV_LORA + LANES:]

    lane = lax.broadcasted_iota(jnp.int32, (t, LANES), 1)
    half = MLA_ROPE // 2
    is_x2 = (lane >= MLA_NOPE + half) & (lane < MLA_NOPE + MLA_ROPE)
    cos_t, sin_t = _rope_patterns(tab_ref[0], lane, TAB_COS_MLA, TAB_SIN_MLA, (MLA_NOPE,), half)

    q = jnp.dot(_rms(c_q, qg_ref[...]).astype(BF16), wq_ref[...], preferred_element_type=F32)
    for hd in range(MLA_HEADS):
        sl = slice(hd * LANES, (hd + 1) * LANES)
        q_out[0, :, sl] = (_rope_slab(q[:, sl], cos_t, sin_t, is_x2, half) * q_scale).astype(BF16)

    kv = jnp.dot(_rms(c_kv, kvg_ref[...]).astype(BF16), wkv_ref[...], preferred_element_type=F32)
    k_rope = _rope_slab(k_rope, cos_t, sin_t, is_x2, half)
    for hd in range(MLA_HEADS):
        sl = slice(hd * LANES, (hd + 1) * LANES)
        k_out[0, :, sl] = (kv[:, sl] + k_rope).astype(BF16)
    v_out[0] = kv[:, MLA_HEADS * LANES:].astype(BF16)

    ext = jnp.concatenate([halo_sc[...], u], axis=0)
    halo_sc[...] = u[t - POOL_HALO:, :]
    tok = si * t + lax.broadcasted_iota(jnp.int32, (t, 1), 0)
    for g, win in enumerate(POOL_WINDOWS):
        sl = slice(g * POOL_GROUP, (g + 1) * POOL_GROUP)
        e = ext[:, sl]
        shift = 1
        while shift < win:
            e = e + pltpu.roll(e, shift, 0)
            shift *= 2
        cnt = jnp.minimum(tok + 1, win).astype(F32)
        pooled = e[POOL_HALO:] * (1.0 / cnt) - u[:, sl]
        y = jnp.dot(pooled.astype(BF16), pw_ref[g], preferred_element_type=F32) + pb_ref[:, sl]
        pool_out[0, :, sl] = (y * ps_ref[:, sl]).astype(BF16)


def _l0_pre(x, mod, n1g, w_in_ext, qg, kvg, wq, wkv, rope_tab, pw, pb, ps):
    b, s, d = x.shape
    t = TOK_TILE
    tok = lambda w: pl.BlockSpec((1, t, w), lambda i, j: (i, j, 0))
    q_scale = (MLA_NOPE + MLA_ROPE) ** -0.5 * LOG2E
    return pl.pallas_call(
        functools.partial(_l0_pre_kernel, q_scale=q_scale),
        grid=(b, s // t),
        in_specs=[
            tok(d),
            pl.BlockSpec((1, 6, d), lambda i, j: (i, 0, 0)),
            _const_spec(n1g.shape), _const_spec(w_in_ext.shape), _const_spec(qg.shape), _const_spec(kvg.shape),
            _const_spec(wq.shape), _const_spec(wkv.shape),
            tok(LANES),
            _const_spec(pw.shape), _const_spec(pb.shape), _const_spec(ps.shape),
        ],
        out_specs=[tok(MLA_HEADS * LANES), tok(MLA_HEADS * LANES), tok(MLA_HEADS * MLA_V), tok(POOL_WIDTH)],
        out_shape=[
            jax.ShapeDtypeStruct((b, s, MLA_HEADS * LANES), BF16),
            jax.ShapeDtypeStruct((b, s, MLA_HEADS * LANES), BF16),
            jax.ShapeDtypeStruct((b, s, MLA_HEADS * MLA_V), BF16),
            jax.ShapeDtypeStruct((b, s, POOL_WIDTH), BF16),
        ],
        scratch_shapes=[pltpu.VMEM((POOL_HALO, POOL_WIDTH), F32)],
        compiler_params=pltpu.CompilerParams(
            dimension_semantics=("arbitrary", "arbitrary"), vmem_limit_bytes=VMEM_LIMIT),
        name="l0_pre",
    )(x, mod, n1g, w_in_ext, qg, kvg, wq, wkv, rope_tab, pw, pb, ps)


def _l1_pre_kernel(x_ref, mod_ref, n1g_ref, wqkv_ref, tab_ref, q_out, k_out, v_out, *, q_scale):
    t = x_ref.shape[1]
    mod = mod_ref[0]
    h = (_rms(x_ref[0], n1g_ref[...]) * (1.0 + mod[1:2]) + mod[0:1]).astype(BF16)
    lane = lax.broadcasted_iota(jnp.int32, (t, LANES), 1)
    half = ROT_DIFF // 2
    is_x2 = ((lane % DIFF_HD) >= half) & ((lane % DIFF_HD) < ROT_DIFF)
    cos_t, sin_t = _rope_patterns(tab_ref[0], lane, TAB_COS_DIFF, TAB_SIN_DIFF, (0, DIFF_HD), half)
    width = DIFF_HEADS * LANES
    q = jnp.dot(h, wqkv_ref[:, :width], preferred_element_type=F32)
    for hd in range(DIFF_HEADS):
        sl = slice(hd * LANES, (hd + 1) * LANES)
        q_out[0, :, sl] = (_rope_slab(q[:, sl], cos_t, sin_t, is_x2, half) * q_scale).astype(BF16)
    k = jnp.dot(h, wqkv_ref[:, width:2 * width], preferred_element_type=F32)
    for hd in range(DIFF_HEADS):
        sl = slice(hd * LANES, (hd + 1) * LANES)
        k_out[0, :, sl] = _rope_slab(k[:, sl], cos_t, sin_t, is_x2, half).astype(BF16)
    v_out[0] = jnp.dot(h, wqkv_ref[:, 2 * width:], preferred_element_type=F32).astype(BF16)


def _l1_pre(x, mod, n1g, wqkv, rope_tab):
    b, s, d = x.shape
    t = TOK_TILE
    tok = lambda w: pl.BlockSpec((1, t, w), lambda i, j: (i, j, 0))
    width = DIFF_HEADS * LANES
    return pl.pallas_call(
        functools.partial(_l1_pre_kernel, q_scale=DIFF_HD ** -0.5 * LOG2E),
        grid=(b, s // t),
        in_specs=[
            tok(d),
            pl.BlockSpec((1, 6, d), lambda i, j: (i, 0, 0)),
            _const_spec(n1g.shape), _const_spec(wqkv.shape),
            tok(LANES),
        ],
        out_specs=[tok(width)] * 3,
        out_shape=[jax.ShapeDtypeStruct((b, s, width), BF16)] * 3,
        compiler_params=pltpu.CompilerParams(
            dimension_semantics=("arbitrary", "arbitrary"), vmem_limit_bytes=VMEM_LIMIT),
        name="l1_pre",
    )(x, mod, n1g, wqkv, rope_tab)


def _flash_group(qs, k_ats, v_ats, n_full, row_pos, m_sc, acc_sc):
    n_grp = len(qs)
    m_sc[...] = jnp.full_like(m_sc, MASK_VALUE)
    acc_sc[...] = jnp.zeros_like(acc_sc)
    n_blk = ATTN_TK // LANES
    acc_blk = acc_sc.shape[-1] // LANES

    def step(j, masked):
        for g in range(n_grp):
            s = lax.dot_general(qs[g], k_ats[g](j), (((1,), (1,)), ((), ())), preferred_element_type=F32)
            if masked:
                col = j * ATTN_TK + lax.broadcasted_iota(jnp.int32, s.shape, 1)
                s = jnp.where(col <= row_pos, s, MASK_VALUE)
            blocks = [s[:, c * LANES:(c + 1) * LANES] for c in range(n_blk)]
            m_blk = blocks[0]
            for blk in blocks[1:]:
                m_blk = jnp.maximum(m_blk, blk)
            m_prev = m_sc[g]
            m_new = jnp.maximum(m_prev, jnp.max(m_blk, axis=1, keepdims=True))
            alpha = jnp.exp2(m_prev - m_new)
            p = jnp.concatenate([jnp.exp2(blk - m_new).astype(BF16) for blk in blocks], axis=1)
            pv = jnp.dot(p, v_ats[g](j), preferred_element_type=F32)
            alpha_w = alpha if acc_blk == 1 else jnp.concatenate([alpha] * acc_blk, axis=1)
            acc_sc[g] = alpha_w * acc_sc[g] + pv
            m_sc[g] = m_new

    def body(j, carry):
        step(j, False)
        return carry

    lax.fori_loop(0, n_full, body, 0)
    step(n_full, True)


def _key_step(ref, lanes):
    return lambda j: ref[0, pl.ds(pl.multiple_of(j * ATTN_TK, ATTN_TK), ATTN_TK), lanes]


def _mla_attn_kernel(q_ref, k_ref, v_ref, o_ref, m_sc, acc_sc):
    tq = q_ref.shape[1]
    qi = pl.program_id(1)
    n_full = (qi * tq) // ATTN_TK
    row_pos = qi * tq + lax.broadcasted_iota(jnp.int32, (tq, 1), 0)
    lane = lax.broadcasted_iota(jnp.int32, (tq, LANES), 1)
    one = jnp.ones((ATTN_TK, LANES), BF16)
    n_grp = m_sc.shape[0]

    def value_step(hd):
        raw = _key_step(v_ref, slice((hd // 2) * LANES, (hd // 2 + 1) * LANES))
        return lambda j: jnp.concatenate([raw(j), one], axis=1)

    for first in range(0, MLA_HEADS, n_grp):
        heads = range(first, first + n_grp)
        slabs = [slice(hd * LANES, (hd + 1) * LANES) for hd in heads]
        _flash_group([q_ref[0, :, sl] for sl in slabs], [_key_step(k_ref, sl) for sl in slabs],
                     [value_step(hd) for hd in heads], n_full, row_pos, m_sc, acc_sc)
        for g in range(0, n_grp, 2):
            even = acc_sc[g, :, :LANES] / acc_sc[g, :, LANES:LANES + 1]
            odd = acc_sc[g + 1, :, :LANES] / acc_sc[g + 1, :, LANES:LANES + 1]
            pair = (first + g) // 2
            o_ref[0, :, pair * LANES:(pair + 1) * LANES] = jnp.where(lane < MLA_V, even, odd).astype(BF16)


def _mla_attention(q, k, v):
    b, s, _ = q.shape
    tq = MLA_TQ
    return pl.pallas_call(
        _mla_attn_kernel,
        grid=(b, s // tq),
        in_specs=[
            pl.BlockSpec((1, tq, q.shape[2]), lambda i, j: (i, j, 0)),
            pl.BlockSpec((1, s, k.shape[2]), lambda i, j: (i, 0, 0)),
            pl.BlockSpec((1, s, v.shape[2]), lambda i, j: (i, 0, 0)),
        ],
        out_specs=pl.BlockSpec((1, tq, v.shape[2]), lambda i, j: (i, j, 0)),
        out_shape=jax.ShapeDtypeStruct(v.shape, BF16),
        scratch_shapes=[pltpu.VMEM((ATTN_GROUP, tq, LANES), F32), pltpu.VMEM((ATTN_GROUP, tq, 2 * LANES), F32)],
        compiler_params=pltpu.CompilerParams(
            dimension_semantics=("arbitrary", "arbitrary"), vmem_limit_bytes=VMEM_LIMIT),
        name="mla_attention",
    )(q, k, v)


def _diff_attn_kernel(q_ref, k_ref, v_ref, lam_ref, g_ref, o_ref, m_sc, acc_sc, *, lambda_init):
    tq = q_ref.shape[1]
    qi = pl.program_id(1)
    n_full = (qi * tq) // ATTN_TK
    row = lax.broadcasted_iota(jnp.int32, (2 * tq, 1), 0)
    row_pos = qi * tq + jnp.where(row >= tq, row - tq, row)
    lane = lax.broadcasted_iota(jnp.int32, (tq, LANES), 1)
    one = jnp.ones((ATTN_TK, LANES), BF16)
    lv = lam_ref[...]
    lam = (jnp.exp(jnp.sum(lv[0:1] * lv[1:2], axis=1, keepdims=True))
           - jnp.exp(jnp.sum(lv[2:3] * lv[3:4], axis=1, keepdims=True)) + lambda_init)
    n_grp = m_sc.shape[0]
    for first in range(0, DIFF_HEADS, n_grp):
        slabs = [slice(hd * LANES, (hd + 1) * LANES) for hd in range(first, first + n_grp)]
        qs = []
        for sl in slabs:
            qh = q_ref[0, :, sl]
            zero = jnp.zeros_like(qh)
            qs.append(jnp.concatenate(
                [jnp.where(lane < DIFF_HD, qh, zero), jnp.where(lane >= DIFF_HD, qh, zero)], axis=0))
        v_ats = [lambda j, raw=_key_step(v_ref, sl): jnp.concatenate([raw(j), one], axis=1) for sl in slabs]
        _flash_group(qs, [_key_step(k_ref, sl) for sl in slabs], v_ats, n_full, row_pos, m_sc, acc_sc)
        for g, sl in enumerate(slabs):
            o = acc_sc[g, :, :LANES] / acc_sc[g, :, LANES:LANES + 1]
            o = o[:tq] - lam * o[tq:]
            o_ref[0, :, sl] = (_rms(o, g_ref[...]) * (1.0 - lambda_init)).astype(BF16)


def _diff_attention(q, k, v, lam_vecs, subln_g, lambda_init):
    b, s, w = q.shape
    tq = DIFF_TQ
    return pl.pallas_call(
        functools.partial(_diff_attn_kernel, lambda_init=lambda_init),
        grid=(b, s // tq),
        in_specs=[
            pl.BlockSpec((1, tq, w), lambda i, j: (i, j, 0)),
            pl.BlockSpec((1, s, w), lambda i, j: (i, 0, 0)),
            pl.BlockSpec((1, s, w), lambda i, j: (i, 0, 0)),
            _const_spec(lam_vecs.shape), _const_spec(subln_g.shape),
        ],
        out_specs=pl.BlockSpec((1, tq, w), lambda i, j: (i, j, 0)),
        out_shape=jax.ShapeDtypeStruct((b, s, w), BF16),
        scratch_shapes=[pltpu.VMEM((ATTN_GROUP, 2 * tq, LANES), F32),
                        pltpu.VMEM((ATTN_GROUP, 2 * tq, 2 * LANES), F32)],
        compiler_params=pltpu.CompilerParams(
            dimension_semantics=("arbitrary", "arbitrary"), vmem_limit_bytes=VMEM_LIMIT),
        name="diff_attention",
    )(q, k, v, lam_vecs, subln_g)


def _post_kernel(*refs, n_mix, final_norm):
    mix_refs = refs[:n_mix]
    x_ref, mod_ref, wout_ref, n2g_ref, wgu_ref, wd_ref = refs[n_mix:n_mix + 6]
    rest = refs[n_mix + 6:]
    if final_norm:
        fng_ref, o_ref, acc_sc = rest
    else:
        o_ref, acc_sc = rest
    mod = mod_ref[0]
    y = None
    row = 0
    for r in mix_refs:
        w = r.shape[2]
        part = jnp.dot(r[0], wout_ref[row:row + w, :], preferred_element_type=F32)
        y = part if y is None else y + part
        row += w
    x1 = x_ref[0] + mod[2:3] * y
    h = (_rms(x1, n2g_ref[...]) * (1.0 + mod[4:5]) + mod[3:4]).astype(BF16)
    acc_sc[...] = jnp.zeros_like(acc_sc)

    for c in range(N_FF_CHUNKS):
        cols = slice(c * FF_CHUNK, (c + 1) * FF_CHUNK)
        ucols = slice(D_FF + c * FF_CHUNK, D_FF + (c + 1) * FF_CHUNK)
        g = jnp.dot(h, wgu_ref[:, cols], preferred_element_type=F32)
        u = jnp.dot(h, wgu_ref[:, ucols], preferred_element_type=F32)
        act = (g * jax.nn.sigmoid(g) * u).astype(BF16)
        acc_sc[...] += jnp.dot(act, wd_ref[cols, :], preferred_element_type=F32)
    x2 = x1 + mod[5:6] * acc_sc[...]
    if final_norm:
        x2 = _rms(x2, fng_ref[...])
    o_ref[0] = x2


def _post(mix_inputs, x, mod, wout, n2g, wgu, wd, final_g=None):
    b, s, d = x.shape
    t = POST_TILE
    tok = lambda w: pl.BlockSpec((1, t, w), lambda i, j: (i, j, 0))
    final_norm = final_g is not None
    in_specs = [tok(m.shape[2]) for m in mix_inputs] + [
        tok(d),
        pl.BlockSpec((1, 6, d), lambda i, j: (i, 0, 0)),
        _const_spec(wout.shape), _const_spec(n2g.shape), _const_spec(wgu.shape), _const_spec(wd.shape),
    ]
    args = list(mix_inputs) + [x, mod, wout, n2g, wgu, wd]
    if final_norm:
        in_specs.append(_const_spec(final_g.shape))
        args.append(final_g)
    return pl.pallas_call(
        functools.partial(_post_kernel, n_mix=len(mix_inputs), final_norm=final_norm),
        grid=(b, s // t),
        in_specs=in_specs,
        out_specs=tok(d),
        out_shape=jax.ShapeDtypeStruct((b, s, d), F32),
        scratch_shapes=[pltpu.VMEM((t, d), F32)],
        compiler_params=pltpu.CompilerParams(
            dimension_semantics=("arbitrary", "arbitrary"), vmem_limit_bytes=VMEM_LIMIT),
        name="post_final" if final_norm else "post",
    )(*args)


def _ffn_weights(w_gate_up, w_down):
    return w_gate_up.astype(BF16), w_down.astype(BF16)


def kernel(x, c, positions, ada_w, ada_b, norm1_g, norm2_g, ffn_w_gate_up, ffn_w_down, mla_w_in, mla_q_norm_g,
           mla_kv_norm_g, mla_w_uq, mla_w_ukv, pool_w, pool_b, pool_scale, mix_a_w_out, diff_w_qkv,
           diff_lambda_q1, diff_lambda_k1, diff_lambda_q2, diff_lambda_k2, diff_subln_g, diff_w_out,
           final_norm_g):
    d = D_MODEL
    rope_tab = _rope_table(positions)
    mod = _modulation(c, ada_w, ada_b)

    w_in = mla_w_in[0]
    o_kv, o_kr, o_u = Q_LORA, Q_LORA + KV_LORA, Q_LORA + KV_LORA + MLA_ROPE
    w_kr = jnp.pad(w_in[:, o_kr:o_u], ((0, 0), (MLA_NOPE, LANES - MLA_NOPE - MLA_ROPE)))
    w_in_ext = jnp.concatenate([w_in[:, :o_kr], w_kr, w_in[:, o_u:]], axis=1).astype(BF16)
    qk_dim = MLA_NOPE + MLA_ROPE
    wq = jnp.pad(mla_w_uq[0].reshape(Q_LORA, MLA_HEADS, qk_dim), ((0, 0), (0, 0), (0, LANES - qk_dim)))
    wq = wq.reshape(Q_LORA, MLA_HEADS * LANES).astype(BF16)
    w_ukv = mla_w_ukv[0].reshape(KV_LORA, MLA_HEADS, MLA_NOPE + MLA_V)
    wk = jnp.pad(w_ukv[..., :MLA_NOPE], ((0, 0), (0, 0), (0, LANES - MLA_NOPE))).reshape(KV_LORA, MLA_HEADS * LANES)
    wv = w_ukv[..., MLA_NOPE:].reshape(KV_LORA, MLA_HEADS * MLA_V)
    wkv = jnp.concatenate([wk, wv], axis=1).astype(BF16)
    q0, k0, v0, pool = _l0_pre(
        x, mod[0], norm1_g[0:1], w_in_ext, mla_q_norm_g, mla_kv_norm_g, wq, wkv, rope_tab,
        pool_w[0].astype(BF16), pool_b[0].reshape(1, POOL_WIDTH), pool_scale)
    attn0 = _mla_attention(q0, k0, v0)
    wgu0, wd0 = _ffn_weights(ffn_w_gate_up[0], ffn_w_down[0])
    x = _post([attn0, pool], x, mod[0], mix_a_w_out[0].astype(BF16), norm2_g[0:1], wgu0, wd0)

    lambda_init = 0.8 - 0.6 * math.exp(-0.3 * 1)
    q1, k1, v1 = _l1_pre(x, mod[1], norm1_g[1:2], diff_w_qkv[0].astype(BF16), rope_tab)
    lam_vecs = jnp.concatenate([diff_lambda_q1, diff_lambda_k1, diff_lambda_q2, diff_lambda_k2], axis=0)
    attn1 = _diff_attention(q1, k1, v1, lam_vecs, diff_subln_g, lambda_init)
    wgu1, wd1 = _ffn_weights(ffn_w_gate_up[1], ffn_w_down[1])
    return _post([attn1], x, mod[1], diff_w_out[0].astype(BF16), norm2_g[1:2], wgu1, wd1,
                 final_g=final_norm_g.reshape(1, d))
```

```python
import functools
import math

import jax
import jax.numpy as jnp
from jax import lax
from jax.experimental import pallas as pl
from jax.experimental.pallas import tpu as pltpu

D_MODEL = 1024
EPS = 1e-6
ROPE_THETA = 500000.0
MLA_HEADS = 8
MLA_NOPE = 64
MLA_ROPE = 32
MLA_V = 64
Q_LORA = 384
KV_LORA = 256
POOL_WINDOWS = (2, 4, 8, 16)
POOL_GROUP = 128
POOL_WIDTH = POOL_GROUP * len(POOL_WINDOWS)
DIFF_HEADS = 8
DIFF_HD = 64
ROT_DIFF = DIFF_HD // 4
D_FF = 2816

LANES = 128
POOL_HALO = 16
FF_CHUNK = 256
N_FF_CHUNKS = D_FF // FF_CHUNK
assert N_FF_CHUNKS * FF_CHUNK == D_FF

TOK_TILE = 512
POST_TILE = 1024
MLA_TQ = 512
DIFF_TQ = 256
ATTN_TK = 512
ATTN_GROUP = 8
VMEM_LIMIT = 56 * 1024 * 1024

LOG2E = 1.4426950408889634
MASK_VALUE = -1e30

F32 = jnp.float32
BF16 = jnp.bfloat16


def _const_spec(shape):
    nd = len(shape)
    return pl.BlockSpec(shape, lambda *_: (0,) * nd, pipeline_mode=pl.Buffered(1))


def _rms(x, g):
    return x * lax.rsqrt(jnp.mean(x * x, axis=-1, keepdims=True) + EPS) * g


def _rope_slab(x, cos_t, sin_t, is_x2, half):
    partner = jnp.where(is_x2, pltpu.roll(x, half, 1), pltpu.roll(x, LANES - half, 1))
    return x * cos_t + partner * sin_t


TAB_FREQS = 32
TAB_COS_MLA, TAB_COS_DIFF = 0, 16
TAB_SIN_MLA, TAB_SIN_DIFF = TAB_FREQS, TAB_FREQS + 16
TAB_SLOTS = LANES // TAB_FREQS


def _rope_table_kernel(pos_ref, inv_ref, tab_ref):
    ang = pos_ref[...].astype(F32) * inv_ref[...]
    cos = jnp.cos(ang)
    sin = jnp.sin(ang)
    lane = lax.broadcasted_iota(jnp.int32, ang.shape, 1)
    for slot in range(TAB_SLOTS):
        shift_c = (LANES - slot * TAB_FREQS) % LANES
        shift_s = (LANES + TAB_FREQS - slot * TAB_FREQS) % LANES
        c = cos if shift_c == 0 else pltpu.roll(cos, shift_c, 1)
        s = sin if shift_s == 0 else pltpu.roll(sin, shift_s, 1)
        tab_ref[slot] = jnp.where(lane < TAB_FREQS, c, jnp.where(lane < 2 * TAB_FREQS, s, 0.0))


def _rope_table(positions):
    b, s = positions.shape
    rows = b * s // TAB_SLOTS
    inv_mla = ROPE_THETA ** (-jnp.arange(0, MLA_ROPE, 2, dtype=F32) / MLA_ROPE)
    inv_diff = ROPE_THETA ** (-jnp.arange(0, ROT_DIFF, 2, dtype=F32) / ROT_DIFF)
    inv = jnp.concatenate([inv_mla, inv_diff, jnp.zeros((TAB_FREQS - 24,), F32)])
    inv = jnp.tile(inv, TAB_SLOTS).reshape(1, LANES)
    pos = jnp.broadcast_to(positions.reshape(TAB_SLOTS, rows).T[:, :, None], (rows, TAB_SLOTS, TAB_FREQS))
    pos = pos.reshape(rows, LANES)
    blk = 1024
    tab = pl.pallas_call(
        _rope_table_kernel,
        grid=(rows // blk,),
        in_specs=[pl.BlockSpec((blk, LANES), lambda i: (i, 0)), _const_spec((1, LANES))],
        out_specs=pl.BlockSpec((TAB_SLOTS, blk, LANES), lambda i: (0, i, 0)),
        out_shape=jax.ShapeDtypeStruct((TAB_SLOTS, rows, LANES), F32),
        name="rope_table",
    )(pos, inv)
    return tab.reshape(b, s, LANES)


def _lane_pattern(tab, lane, pieces, default):
    out = jnp.full_like(tab, default)
    for lo, hi, src, sign in pieces:
        rolled = pltpu.roll(tab, (lo - src) % LANES, 1)
        out = jnp.where((lane >= lo) & (lane < hi), -rolled if sign < 0 else rolled, out)
    return out


def _rope_patterns(tab, lane, cos_src, sin_src, groups, half):
    cos_pieces, sin_pieces = [], []
    for g in groups:
        cos_pieces += [(g, g + half, cos_src, 1), (g + half, g + 2 * half, cos_src, 1)]
        sin_pieces += [(g, g + half, sin_src, -1), (g + half, g + 2 * half, sin_src, 1)]
    return _lane_pattern(tab, lane, cos_pieces, 1.0), _lane_pattern(tab, lane, sin_pieces, 0.0)


def _mod_kernel(c_ref, w_ref, b_ref, o_ref):
    c = c_ref[...]
    cond = (c * jax.nn.sigmoid(c)).astype(BF16)
    o_ref[0] = jnp.dot(cond, w_ref[0].astype(BF16), preferred_element_type=F32) + b_ref[0]


def _modulation(c, ada_w, ada_b):
    depth, d, n = ada_w.shape
    b = c.shape[0]
    blk = 1536
    mod = pl.pallas_call(
        _mod_kernel,
        grid=(depth, n // blk),
        in_specs=[
            _const_spec((b, d)),
            pl.BlockSpec((1, d, blk), lambda i, j: (i, 0, j)),
            pl.BlockSpec((1, 1, blk), lambda i, j: (i, 0, j)),
        ],
        out_specs=pl.BlockSpec((1, b, blk), lambda i, j: (i, 0, j)),
        out_shape=jax.ShapeDtypeStruct((depth, b, n), F32),
        compiler_params=pltpu.CompilerParams(vmem_limit_bytes=VMEM_LIMIT),
        name="adaln_mod",
    )(c, ada_w, ada_b.reshape(depth, 1, n))
    return mod.reshape(depth, b, 6, d)


def _l0_pre_kernel(x_ref, mod_ref, n1g_ref, win_ref, qg_ref, kvg_ref, wq_ref, wkv_ref, tab_ref,
                   pw_ref, pb_ref, ps_ref, q_out, k_out, v_out, pool_out, halo_sc, *, q_scale):
    si = pl.program_id(1)
    t = x_ref.shape[1]

    @pl.when(si == 0)
    def _():
        halo_sc[...] = jnp.zeros_like(halo_sc)

    mod = mod_ref[0]
    h = _rms(x_ref[0], n1g_ref[...]) * (1.0 + mod[1:2]) + mod[0:1]
    proj = jnp.dot(h.astype(BF16), win_ref[...], preferred_element_type=F32)
    c_q = proj[:, :Q_LORA]
    c_kv = proj[:, Q_LORA:Q_LORA + KV_LORA]
    k_rope = proj[:, Q_LORA + KV_LORA:Q_LORA + KV_LORA + LANES]
    u = proj[:, Q_LORA + KV_LORA + LANES:]

    lane = lax.broadcasted_iota(jnp.int32, (t, LANES), 1)
    half = MLA_ROPE // 2
    is_x2 = (lane >= MLA_NOPE + half) & (lane < MLA_NOPE + MLA_ROPE)
    cos_t, sin_t = _rope_patterns(tab_ref[0], lane, TAB_COS_MLA, TAB_SIN_MLA, (MLA_NOPE,), half)

    q = jnp.dot(_rms(c_q, qg_ref[...]).astype(BF16), wq_ref[...], preferred_element_type=F32)
    for hd in range(MLA_HEADS):
        sl = slice(hd * LANES, (hd + 1) * LANES)
        q_out[0, :, sl] = (_rope_slab(q[:, sl], cos_t, sin_t, is_x2, half) * q_scale).astype(BF16)

    kv = jnp.dot(_rms(c_kv, kvg_ref[...]).astype(BF16), wkv_ref[...], preferred_element_type=F32)
    k_rope = _rope_slab(k_rope, cos_t, sin_t, is_x2, half)
    for hd in range(MLA_HEADS):
        sl = slice(hd * LANES, (hd + 1) * LANES)
        k_out[0, :, sl] = (kv[:, sl] + k_rope).astype(BF16)
    v_out[0] = kv[:, MLA_HEADS * LANES:].astype(BF16)

    ext = jnp.concatenate([halo_sc[...], u], axis=0)
    halo_sc[...] = u[t - POOL_HALO:, :]
    tok = si * t + lax.broadcasted_iota(jnp.int32, (t, 1), 0)
    for g, win in enumerate(POOL_WINDOWS):
        sl = slice(g * POOL_GROUP, (g + 1) * POOL_GROUP)
        e = ext[:, sl]
        shift = 1
        while shift < win:
            e = e + pltpu.roll(e, shift, 0)
            shift *= 2
        cnt = jnp.minimum(tok + 1, win).astype(F32)
        pooled = e[POOL_HALO:] * (1.0 / cnt) - u[:, sl]
        y = jnp.dot(pooled.astype(BF16), pw_ref[g], preferred_element_type=F32) + pb_ref[:, sl]
        pool_out[0, :, sl] = (y * ps_ref[:, sl]).astype(BF16)


def _l0_pre(x, mod, n1g, w_in_ext, qg, kvg, wq, wkv, rope_tab, pw, pb, ps):
    b, s, d = x.shape
    t = TOK_TILE
    tok = lambda w: pl.BlockSpec((1, t, w), lambda i, j: (i, j, 0))
    q_scale = (MLA_NOPE + MLA_ROPE) ** -0.5 * LOG2E
    return pl.pallas_call(
        functools.partial(_l0_pre_kernel, q_scale=q_scale),
        grid=(b, s // t),
        in_specs=[
            tok(d),
            pl.BlockSpec((1, 6, d), lambda i, j: (i, 0, 0)),
            _const_spec(n1g.shape), _const_spec(w_in_ext.shape), _const_spec(qg.shape), _const_spec(kvg.shape),
            _const_spec(wq.shape), _const_spec(wkv.shape),
            tok(LANES),
            _const_spec(pw.shape), _const_spec(pb.shape), _const_spec(ps.shape),
        ],
        out_specs=[tok(MLA_HEADS * LANES), tok(MLA_HEADS * LANES), tok(MLA_HEADS * MLA_V), tok(POOL_WIDTH)],
        out_shape=[
            jax.ShapeDtypeStruct((b, s, MLA_HEADS * LANES), BF16),
            jax.ShapeDtypeStruct((b, s, MLA_HEADS * LANES), BF16),
            jax.ShapeDtypeStruct((b, s, MLA_HEADS * MLA_V), BF16),
            jax.ShapeDtypeStruct((b, s, POOL_WIDTH), BF16),
        ],
        scratch_shapes=[pltpu.VMEM((POOL_HALO, POOL_WIDTH), F32)],
        compiler_params=pltpu.CompilerParams(
            dimension_semantics=("arbitrary", "arbitrary"), vmem_limit_bytes=VMEM_LIMIT),
        name="l0_pre",
    )(x, mod, n1g, w_in_ext, qg, kvg, wq, wkv, rope_tab, pw, pb, ps)


def _l1_pre_kernel(x_ref, mod_ref, n1g_ref, wqkv_ref, tab_ref, q_out, k_out, v_out, *, q_scale):
    t = x_ref.shape[1]
    mod = mod_ref[0]
    h = (_rms(x_ref[0], n1g_ref[...]) * (1.0 + mod[1:2]) + mod[0:1]).astype(BF16)
    lane = lax.broadcasted_iota(jnp.int32, (t, LANES), 1)
    half = ROT_DIFF // 2
    is_x2 = ((lane % DIFF_HD) >= half) & ((lane % DIFF_HD) < ROT_DIFF)
    cos_t, sin_t = _rope_patterns(tab_ref[0], lane, TAB_COS_DIFF, TAB_SIN_DIFF, (0, DIFF_HD), half)
    width = DIFF_HEADS * LANES
    q = jnp.dot(h, wqkv_ref[:, :width], preferred_element_type=F32)
    for hd in range(DIFF_HEADS):
        sl = slice(hd * LANES, (hd + 1) * LANES)
        q_out[0, :, sl] = (_rope_slab(q[:, sl], cos_t, sin_t, is_x2, half) * q_scale).astype(BF16)
    k = jnp.dot(h, wqkv_ref[:, width:2 * width], preferred_element_type=F32)
    for hd in range(DIFF_HEADS):
        sl = slice(hd * LANES, (hd + 1) * LANES)
        k_out[0, :, sl] = _rope_slab(k[:, sl], cos_t, sin_t, is_x2, half).astype(BF16)
    v_out[0] = jnp.dot(h, wqkv_ref[:, 2 * width:], preferred_element_type=F32).astype(BF16)


def _l1_pre(x, mod, n1g, wqkv, rope_tab):
    b, s, d = x.shape
    t = TOK_TILE
    tok = lambda w: pl.BlockSpec((1, t, w), lambda i, j: (i, j, 0))
    width = DIFF_HEADS * LANES
    return pl.pallas_call(
        functools.partial(_l1_pre_kernel, q_scale=DIFF_HD ** -0.5 * LOG2E),
        grid=(b, s // t),
        in_specs=[
            tok(d),
            pl.BlockSpec((1, 6, d), lambda i, j: (i, 0, 0)),
            _const_spec(n1g.shape), _const_spec(wqkv.shape),
            tok(LANES),
        ],
        out_specs=[tok(width)] * 3,
        out_shape=[jax.ShapeDtypeStruct((b, s, width), BF16)] * 3,
        compiler_params=pltpu.CompilerParams(
            dimension_semantics=("arbitrary", "arbitrary"), vmem_limit_bytes=VMEM_LIMIT),
        name="l1_pre",
    )(x, mod, n1g, wqkv, rope_tab)


def _flash_group(qs, k_ats, v_ats, n_full, row_pos, m_sc, acc_sc, p_sc, alpha_sc):
    n_grp = len(qs)
    n_blk = ATTN_TK // LANES
    acc_blk = acc_sc.shape[-1] // LANES
    acc_sc[...] = jnp.zeros_like(acc_sc)
    col = n_full * ATTN_TK + lax.broadcasted_iota(jnp.int32, (qs[0].shape[0], ATTN_TK), 1)
    mask_bias = jnp.where(col <= row_pos, 0.0, MASK_VALUE)

    def scores(g, j, first):
        s = lax.dot_general(qs[g], k_ats[g](j), (((1,), (1,)), ((), ())), preferred_element_type=F32)
        if first:
            s = s + mask_bias
        blocks = [s[:, c * LANES:(c + 1) * LANES] for c in range(n_blk)]
        m_blk = blocks[0]
        for blk in blocks[1:]:
            m_blk = jnp.maximum(m_blk, blk)
        m_new = jnp.max(m_blk, axis=1, keepdims=True)
        if first:
            m_new = jnp.broadcast_to(m_new, m_sc.shape[1:])
            alpha_sc[g] = jnp.zeros_like(m_new)
        else:
            m_prev = m_sc[g]
            m_new = jnp.maximum(m_prev, m_new)
            alpha_sc[g] = jnp.exp2(m_prev - m_new)
        p_sc[g] = jnp.concatenate([jnp.exp2(blk - m_new).astype(BF16) for blk in blocks], axis=1)
        m_sc[g] = m_new

    def values(g, j):
        pv = jnp.dot(p_sc[g], v_ats[g](j), preferred_element_type=F32)
        alpha = alpha_sc[g]
        alpha_w = alpha if acc_blk == 1 else jnp.concatenate([alpha] * acc_blk, axis=1)
        acc_sc[g] = alpha_w * acc_sc[g] + pv

    for g in range(n_grp):
        scores(g, n_full, True)

    def body(j, carry):
        prev = jnp.where(j == 0, n_full, j - 1)
        for g in range(n_grp):
            values(g, prev)
            scores(g, j, False)
        return carry

    lax.fori_loop(0, n_full, body, 0)
    last = jnp.maximum(n_full - 1, 0)
    for g in range(n_grp):
        values(g, last)


def _flash_scratch(rows, acc_width):
    return [pltpu.VMEM((ATTN_GROUP, rows, LANES), F32), pltpu.VMEM((ATTN_GROUP, rows, acc_width), F32),
            pltpu.VMEM((ATTN_GROUP, rows, ATTN_TK), BF16), pltpu.VMEM((ATTN_GROUP, rows, LANES), F32)]


def _key_step(ref, lanes):
    return lambda j: ref[0, pl.ds(pl.multiple_of(j * ATTN_TK, ATTN_TK), ATTN_TK), lanes]


def _mla_attn_kernel(q_ref, k_ref, v_ref, o_ref, m_sc, acc_sc, p_sc, alpha_sc):
    tq = q_ref.shape[1]
    qi = pl.program_id(1)
    n_full = (qi * tq) // ATTN_TK
    row_pos = qi * tq + lax.broadcasted_iota(jnp.int32, (tq, 1), 0)
    lane = lax.broadcasted_iota(jnp.int32, (tq, LANES), 1)
    one = jnp.ones((ATTN_TK, LANES), BF16)
    n_grp = m_sc.shape[0]

    def value_step(hd):
        raw = _key_step(v_ref, slice((hd // 2) * LANES, (hd // 2 + 1) * LANES))
        return lambda j: jnp.concatenate([raw(j), one], axis=1)

    for first in range(0, MLA_HEADS, n_grp):
        heads = range(first, first + n_grp)
        slabs = [slice(hd * LANES, (hd + 1) * LANES) for hd in heads]
        _flash_group([q_ref[0, :, sl] for sl in slabs], [_key_step(k_ref, sl) for sl in slabs],
                     [value_step(hd) for hd in heads], n_full, row_pos, m_sc, acc_sc, p_sc, alpha_sc)
        for g in range(0, n_grp, 2):
            even = acc_sc[g, :, :LANES] / acc_sc[g, :, LANES:LANES + 1]
            odd = acc_sc[g + 1, :, :LANES] / acc_sc[g + 1, :, LANES:LANES + 1]
            pair = (first + g) // 2
            o_ref[0, :, pair * LANES:(pair + 1) * LANES] = jnp.where(lane < MLA_V, even, odd).astype(BF16)


def _mla_attention(q, k, v):
    b, s, _ = q.shape
    tq = MLA_TQ
    return pl.pallas_call(
        _mla_attn_kernel,
        grid=(b, s // tq),
        in_specs=[
            pl.BlockSpec((1, tq, q.shape[2]), lambda i, j: (i, j, 0)),
            pl.BlockSpec((1, s, k.shape[2]), lambda i, j: (i, 0, 0)),
            pl.BlockSpec((1, s, v.shape[2]), lambda i, j: (i, 0, 0)),
        ],
        out_specs=pl.BlockSpec((1, tq, v.shape[2]), lambda i, j: (i, j, 0)),
        out_shape=jax.ShapeDtypeStruct(v.shape, BF16),
        scratch_shapes=_flash_scratch(tq, 2 * LANES),
        compiler_params=pltpu.CompilerParams(
            dimension_semantics=("arbitrary", "arbitrary"), vmem_limit_bytes=VMEM_LIMIT),
        name="mla_attention",
    )(q, k, v)


def _diff_attn_kernel(q_ref, k_ref, v_ref, lam_ref, g_ref, o_ref, m_sc, acc_sc, p_sc, alpha_sc, *, lambda_init):
    tq = q_ref.shape[1]
    qi = pl.program_id(1)
    n_full = (qi * tq) // ATTN_TK
    row = lax.broadcasted_iota(jnp.int32, (2 * tq, 1), 0)
    row_pos = qi * tq + jnp.where(row >= tq, row - tq, row)
    lane = lax.broadcasted_iota(jnp.int32, (tq, LANES), 1)
    one = jnp.ones((ATTN_TK, LANES), BF16)
    lv = lam_ref[...]
    lam = (jnp.exp(jnp.sum(lv[0:1] * lv[1:2], axis=1, keepdims=True))
           - jnp.exp(jnp.sum(lv[2:3] * lv[3:4], axis=1, keepdims=True)) + lambda_init)
    n_grp = m_sc.shape[0]
    for first in range(0, DIFF_HEADS, n_grp):
        slabs = [slice(hd * LANES, (hd + 1) * LANES) for hd in range(first, first + n_grp)]
        qs = []
        for sl in slabs:
            qh = q_ref[0, :, sl]
            zero = jnp.zeros_like(qh)
            qs.append(jnp.concatenate(
                [jnp.where(lane < DIFF_HD, qh, zero), jnp.where(lane >= DIFF_HD, qh, zero)], axis=0))
        v_ats = [lambda j, raw=_key_step(v_ref, sl): jnp.concatenate([raw(j), one], axis=1) for sl in slabs]
        _flash_group(qs, [_key_step(k_ref, sl) for sl in slabs], v_ats, n_full, row_pos,
                     m_sc, acc_sc, p_sc, alpha_sc)
        for g, sl in enumerate(slabs):
            o = acc_sc[g, :, :LANES] / acc_sc[g, :, LANES:LANES + 1]
            o = o[:tq] - lam * o[tq:]
            o_ref[0, :, sl] = (_rms(o, g_ref[...]) * (1.0 - lambda_init)).astype(BF16)


def _diff_attention(q, k, v, lam_vecs, subln_g, lambda_init):
    b, s, w = q.shape
    tq = DIFF_TQ
    return pl.pallas_call(
        functools.partial(_diff_attn_kernel, lambda_init=lambda_init),
        grid=(b, s // tq),
        in_specs=[
            pl.BlockSpec((1, tq, w), lambda i, j: (i, j, 0)),
            pl.BlockSpec((1, s, w), lambda i, j: (i, 0, 0)),
            pl.BlockSpec((1, s, w), lambda i, j: (i, 0, 0)),
            _const_spec(lam_vecs.shape), _const_spec(subln_g.shape),
        ],
        out_specs=pl.BlockSpec((1, tq, w), lambda i, j: (i, j, 0)),
        out_shape=jax.ShapeDtypeStruct((b, s, w), BF16),
        scratch_shapes=_flash_scratch(2 * tq, 2 * LANES),
        compiler_params=pltpu.CompilerParams(
            dimension_semantics=("arbitrary", "arbitrary"), vmem_limit_bytes=VMEM_LIMIT),
        name="diff_attention",
    )(q, k, v, lam_vecs, subln_g)


def _post_kernel(*refs, n_mix, final_norm):
    mix_refs = refs[:n_mix]
    x_ref, mod_ref, wout_ref, n2g_ref, wgu_ref, wd_ref = refs[n_mix:n_mix + 6]
    rest = refs[n_mix + 6:]
    if final_norm:
        fng_ref, o_ref, acc_sc = rest
    else:
        o_ref, acc_sc = rest
    mod = mod_ref[0]
    y = None
    row = 0
    for r in mix_refs:
        w = r.shape[2]
        part = jnp.dot(r[0], wout_ref[row:row + w, :], preferred_element_type=F32)
        y = part if y is None else y + part
        row += w
    x1 = x_ref[0] + mod[2:3] * y
    h = (_rms(x1, n2g_ref[...]) * (1.0 + mod[4:5]) + mod[3:4]).astype(BF16)
    acc_sc[...] = jnp.zeros_like(acc_sc)

    for c in range(N_FF_CHUNKS):
        cols = slice(c * FF_CHUNK, (c + 1) * FF_CHUNK)
        ucols = slice(D_FF + c * FF_CHUNK, D_FF + (c + 1) * FF_CHUNK)
        g = jnp.dot(h, wgu_ref[:, cols], preferred_element_type=F32)
        u = jnp.dot(h, wgu_ref[:, ucols], preferred_element_type=F32)
        act = (g * jax.nn.sigmoid(g) * u).astype(BF16)
        acc_sc[...] += jnp.dot(act, wd_ref[cols, :], preferred_element_type=F32)
    x2 = x1 + mod[5:6] * acc_sc[...]
    if final_norm:
        x2 = _rms(x2, fng_ref[...])
    o_ref[0] = x2


def _post(mix_inputs, x, mod, wout, n2g, wgu, wd, final_g=None):
    b, s, d = x.shape
    t = POST_TILE
    tok = lambda w: pl.BlockSpec((1, t, w), lambda i, j: (i, j, 0))
    final_norm = final_g is not None
    in_specs = [tok(m.shape[2]) for m in mix_inputs] + [
        tok(d),
        pl.BlockSpec((1, 6, d), lambda i, j: (i, 0, 0)),
        _const_spec(wout.shape), _const_spec(n2g.shape), _const_spec(wgu.shape), _const_spec(wd.shape),
    ]
    args = list(mix_inputs) + [x, mod, wout, n2g, wgu, wd]
    if final_norm:
        in_specs.append(_const_spec(final_g.shape))
        args.append(final_g)
    return pl.pallas_call(
        functools.partial(_post_kernel, n_mix=len(mix_inputs), final_norm=final_norm),
        grid=(b, s // t),
        in_specs=in_specs,
        out_specs=tok(d),
        out_shape=jax.ShapeDtypeStruct((b, s, d), F32),
        scratch_shapes=[pltpu.VMEM((t, d), F32)],
        compiler_params=pltpu.CompilerParams(
            dimension_semantics=("arbitrary", "arbitrary"), vmem_limit_bytes=VMEM_LIMIT),
        name="post_final" if final_norm else "post",
    )(*args)


def _ffn_weights(w_gate_up, w_down):
    return w_gate_up.astype(BF16), w_down.astype(BF16)


def kernel(x, c, positions, ada_w, ada_b, norm1_g, norm2_g, ffn_w_gate_up, ffn_w_down, mla_w_in, mla_q_norm_g,
           mla_kv_norm_g, mla_w_uq, mla_w_ukv, pool_w, pool_b, pool_scale, mix_a_w_out, diff_w_qkv,
           diff_lambda_q1, diff_lambda_k1, diff_lambda_q2, diff_lambda_k2, diff_subln_g, diff_w_out,
           final_norm_g):
    d = D_MODEL
    rope_tab = _rope_table(positions)
    mod = _modulation(c, ada_w, ada_b)

    w_in = mla_w_in[0]
    o_kv, o_kr, o_u = Q_LORA, Q_LORA + KV_LORA, Q_LORA + KV_LORA + MLA_ROPE
    w_kr = jnp.pad(w_in[:, o_kr:o_u], ((0, 0), (MLA_NOPE, LANES - MLA_NOPE - MLA_ROPE)))
    w_in_ext = jnp.concatenate([w_in[:, :o_kr], w_kr, w_in[:, o_u:]], axis=1).astype(BF16)
    qk_dim = MLA_NOPE + MLA_ROPE
    wq = jnp.pad(mla_w_uq[0].reshape(Q_LORA, MLA_HEADS, qk_dim), ((0, 0), (0, 0), (0, LANES - qk_dim)))
    wq = wq.reshape(Q_LORA, MLA_HEADS * LANES).astype(BF16)
    w_ukv = mla_w_ukv[0].reshape(KV_LORA, MLA_HEADS, MLA_NOPE + MLA_V)
    wk = jnp.pad(w_ukv[..., :MLA_NOPE], ((0, 0), (0, 0), (0, LANES - MLA_NOPE))).reshape(KV_LORA, MLA_HEADS * LANES)
    wv = w_ukv[..., MLA_NOPE:].reshape(KV_LORA, MLA_HEADS * MLA_V)
    wkv = jnp.concatenate([wk, wv], axis=1).astype(BF16)
    q0, k0, v0, pool = _l0_pre(
        x, mod[0], norm1_g[0:1], w_in_ext, mla_q_norm_g, mla_kv_norm_g, wq, wkv, rope_tab,
        pool_w[0].astype(BF16), pool_b[0].reshape(1, POOL_WIDTH), pool_scale)
    attn0 = _mla_attention(q0, k0, v0)
    wgu0, wd0 = _ffn_weights(ffn_w_gate_up[0], ffn_w_down[0])
    x = _post([attn0, pool], x, mod[0], mix_a_w_out[0].astype(BF16), norm2_g[0:1], wgu0, wd0)

    lambda_init = 0.8 - 0.6 * math.exp(-0.3 * 1)
    q1, k1, v1 = _l1_pre(x, mod[1], norm1_g[1:2], diff_w_qkv[0].astype(BF16), rope_tab)
    lam_vecs = jnp.concatenate([diff_lambda_q1, diff_lambda_k1, diff_lambda_q2, diff_lambda_k2], axis=0)
    attn1 = _diff_attention(q1, k1, v1, lam_vecs, diff_subln_g, lambda_init)
    wgu1, wd1 = _ffn_weights(ffn_w_gate_up[1], ffn_w_down[1])
    return _post([attn1], x, mod[1], diff_w_out[0].astype(BF16), norm2_g[1:2], wgu1, wd1,
                 final_g=final_norm_g.reshape(1, d))
```

```python
import functools
import math

import jax
import jax.numpy as jnp
from jax import lax
from jax.experimental import pallas as pl
from jax.experimental.pallas import tpu as pltpu

D_MODEL = 1024
EPS = 1e-6
ROPE_THETA = 500000.0
MLA_HEADS = 8
MLA_NOPE = 64
MLA_ROPE = 32
MLA_V = 64
Q_LORA = 384
KV_LORA = 256
POOL_WINDOWS = (2, 4, 8, 16)
POOL_GROUP = 128
POOL_WIDTH = POOL_GROUP * len(POOL_WINDOWS)
DIFF_HEADS = 8
DIFF_HD = 64
ROT_DIFF = DIFF_HD // 4
D_FF = 2816

LANES = 128
POOL_HALO = 16
FF_CHUNK = 256
N_FF_CHUNKS = D_FF // FF_CHUNK
assert N_FF_CHUNKS * FF_CHUNK == D_FF

TOK_TILE = 1024
POST_TILE = 1024
MLA_TQ = 512
DIFF_TQ = 256
ATTN_TK = 512
ATTN_GROUP = 8
VMEM_LIMIT = 56 * 1024 * 1024

LOG2E = 1.4426950408889634
MASK_VALUE = -1e30

F32 = jnp.float32
BF16 = jnp.bfloat16


def _const_spec(shape):
    nd = len(shape)
    return pl.BlockSpec(shape, lambda *_: (0,) * nd, pipeline_mode=pl.Buffered(1))


def _rms(x, g):
    return x * lax.rsqrt(jnp.mean(x * x, axis=-1, keepdims=True) + EPS) * g


def _rope_slab(x, cos_t, sin_t, is_x2, half):
    partner = jnp.where(is_x2, pltpu.roll(x, half, 1), pltpu.roll(x, LANES - half, 1))
    return x * cos_t + partner * sin_t


TAB_FREQS = 32
TAB_COS_MLA, TAB_COS_DIFF = 0, 16
TAB_SIN_MLA, TAB_SIN_DIFF = TAB_FREQS, TAB_FREQS + 16
TAB_SLOTS = LANES // TAB_FREQS


def _rope_table_kernel(pos_ref, inv_ref, tab_ref):
    ang = pos_ref[...].astype(F32) * inv_ref[...]
    cos = jnp.cos(ang)
    sin = jnp.sin(ang)
    lane = lax.broadcasted_iota(jnp.int32, ang.shape, 1)
    for slot in range(TAB_SLOTS):
        shift_c = (LANES - slot * TAB_FREQS) % LANES
        shift_s = (LANES + TAB_FREQS - slot * TAB_FREQS) % LANES
        c = cos if shift_c == 0 else pltpu.roll(cos, shift_c, 1)
        s = sin if shift_s == 0 else pltpu.roll(sin, shift_s, 1)
        tab_ref[slot] = jnp.where(lane < TAB_FREQS, c, jnp.where(lane < 2 * TAB_FREQS, s, 0.0))


def _rope_table(positions):
    b, s = positions.shape
    rows = b * s // TAB_SLOTS
    inv_mla = ROPE_THETA ** (-jnp.arange(0, MLA_ROPE, 2, dtype=F32) / MLA_ROPE)
    inv_diff = ROPE_THETA ** (-jnp.arange(0, ROT_DIFF, 2, dtype=F32) / ROT_DIFF)
    inv = jnp.concatenate([inv_mla, inv_diff, jnp.zeros((TAB_FREQS - 24,), F32)])
    inv = jnp.tile(inv, TAB_SLOTS).reshape(1, LANES)
    pos = jnp.broadcast_to(positions.reshape(TAB_SLOTS, rows).T[:, :, None], (rows, TAB_SLOTS, TAB_FREQS))
    pos = pos.reshape(rows, LANES)
    blk = 1024
    tab = pl.pallas_call(
        _rope_table_kernel,
        grid=(rows // blk,),
        in_specs=[pl.BlockSpec((blk, LANES), lambda i: (i, 0)), _const_spec((1, LANES))],
        out_specs=pl.BlockSpec((TAB_SLOTS, blk, LANES), lambda i: (0, i, 0)),
        out_shape=jax.ShapeDtypeStruct((TAB_SLOTS, rows, LANES), F32),
        name="rope_table",
    )(pos, inv)
    return tab.reshape(b, s, LANES)


def _lane_pattern(tab, lane, pieces, default):
    out = jnp.full_like(tab, default)
    for lo, hi, src, sign in pieces:
        rolled = pltpu.roll(tab, (lo - src) % LANES, 1)
        out = jnp.where((lane >= lo) & (lane < hi), -rolled if sign < 0 else rolled, out)
    return out


def _rope_patterns(tab, lane, cos_src, sin_src, groups, half):
    cos_pieces, sin_pieces = [], []
    for g in groups:
        cos_pieces += [(g, g + half, cos_src, 1), (g + half, g + 2 * half, cos_src, 1)]
        sin_pieces += [(g, g + half, sin_src, -1), (g + half, g + 2 * half, sin_src, 1)]
    return _lane_pattern(tab, lane, cos_pieces, 1.0), _lane_pattern(tab, lane, sin_pieces, 0.0)


def _mod_kernel(c_ref, w_ref, b_ref, o_ref):
    c = c_ref[...]
    cond = (c * jax.nn.sigmoid(c)).astype(BF16)
    o_ref[0] = jnp.dot(cond, w_ref[0].astype(BF16), preferred_element_type=F32) + b_ref[0]


def _modulation(c, ada_w, ada_b):
    depth, d, n = ada_w.shape
    b = c.shape[0]
    blk = 1536
    mod = pl.pallas_call(
        _mod_kernel,
        grid=(depth, n // blk),
        in_specs=[
            _const_spec((b, d)),
            pl.BlockSpec((1, d, blk), lambda i, j: (i, 0, j)),
            pl.BlockSpec((1, 1, blk), lambda i, j: (i, 0, j)),
        ],
        out_specs=pl.BlockSpec((1, b, blk), lambda i, j: (i, 0, j)),
        out_shape=jax.ShapeDtypeStruct((depth, b, n), F32),
        compiler_params=pltpu.CompilerParams(vmem_limit_bytes=VMEM_LIMIT),
        name="adaln_mod",
    )(c, ada_w, ada_b.reshape(depth, 1, n))
    return mod.reshape(depth, b, 6, d)


def _l0_pre_kernel(x_ref, mod_ref, n1g_ref, win_ref, qg_ref, kvg_ref, wq_ref, wkv_ref, tab_ref,
                   pw_ref, pb_ref, ps_ref, q_out, k_out, v_out, pool_out, halo_sc, *, q_scale):
    si = pl.program_id(1)
    t = x_ref.shape[1]

    @pl.when(si == 0)
    def _():
        halo_sc[...] = jnp.zeros_like(halo_sc)

    mod = mod_ref[0]
    h = _rms(x_ref[0], n1g_ref[...]) * (1.0 + mod[1:2]) + mod[0:1]
    proj = jnp.dot(h.astype(BF16), win_ref[...], preferred_element_type=F32)
    c_q = proj[:, :Q_LORA]
    c_kv = proj[:, Q_LORA:Q_LORA + KV_LORA]
    k_rope = proj[:, Q_LORA + KV_LORA:Q_LORA + KV_LORA + LANES]
    u = proj[:, Q_LORA + KV_LORA + LANES:]

    lane = lax.broadcasted_iota(jnp.int32, (t, LANES), 1)
    half = MLA_ROPE // 2
    is_x2 = (lane >= MLA_NOPE + half) & (lane < MLA_NOPE + MLA_ROPE)
    cos_t, sin_t = _rope_patterns(tab_ref[0], lane, TAB_COS_MLA, TAB_SIN_MLA, (MLA_NOPE,), half)

    q = jnp.dot(_rms(c_q, qg_ref[...]).astype(BF16), wq_ref[...], preferred_element_type=F32)
    for hd in range(MLA_HEADS):
        sl = slice(hd * LANES, (hd + 1) * LANES)
        q_out[0, :, sl] = (_rope_slab(q[:, sl], cos_t, sin_t, is_x2, half) * q_scale).astype(BF16)

    kv = jnp.dot(_rms(c_kv, kvg_ref[...]).astype(BF16), wkv_ref[...], preferred_element_type=F32)
    k_rope = _rope_slab(k_rope, cos_t, sin_t, is_x2, half)
    for hd in range(MLA_HEADS):
        sl = slice(hd * LANES, (hd + 1) * LANES)
        k_out[0, :, sl] = (kv[:, sl] + k_rope).astype(BF16)
    v_out[0] = kv[:, MLA_HEADS * LANES:].astype(BF16)

    ext = jnp.concatenate([halo_sc[...], u], axis=0)
    halo_sc[...] = u[t - POOL_HALO:, :]
    tok = si * t + lax.broadcasted_iota(jnp.int32, (t, 1), 0)
    for g, win in enumerate(POOL_WINDOWS):
        sl = slice(g * POOL_GROUP, (g + 1) * POOL_GROUP)
        e = ext[:, sl]
        shift = 1
        while shift < win:
            e = e + pltpu.roll(e, shift, 0)
            shift *= 2
        cnt = jnp.minimum(tok + 1, win).astype(F32)
        pooled = e[POOL_HALO:] * (1.0 / cnt) - u[:, sl]
        y = jnp.dot(pooled.astype(BF16), pw_ref[g], preferred_element_type=F32) + pb_ref[:, sl]
        pool_out[0, :, sl] = (y * ps_ref[:, sl]).astype(BF16)


def _l0_pre(x, mod, n1g, w_in_ext, qg, kvg, wq, wkv, rope_tab, pw, pb, ps):
    b, s, d = x.shape
    t = TOK_TILE
    tok = lambda w: pl.BlockSpec((1, t, w), lambda i, j: (i, j, 0))
    q_scale = (MLA_NOPE + MLA_ROPE) ** -0.5 * LOG2E
    return pl.pallas_call(
        functools.partial(_l0_pre_kernel, q_scale=q_scale),
        grid=(b, s // t),
        in_specs=[
            tok(d),
            pl.BlockSpec((1, 6, d), lambda i, j: (i, 0, 0)),
            _const_spec(n1g.shape), _const_spec(w_in_ext.shape), _const_spec(qg.shape), _const_spec(kvg.shape),
            _const_spec(wq.shape), _const_spec(wkv.shape),
            tok(LANES),
            _const_spec(pw.shape), _const_spec(pb.shape), _const_spec(ps.shape),
        ],
        out_specs=[tok(MLA_HEADS * LANES), tok(MLA_HEADS * LANES), tok(MLA_HEADS * MLA_V), tok(POOL_WIDTH)],
        out_shape=[
            jax.ShapeDtypeStruct((b, s, MLA_HEADS * LANES), BF16),
            jax.ShapeDtypeStruct((b, s, MLA_HEADS * LANES), BF16),
            jax.ShapeDtypeStruct((b, s, MLA_HEADS * MLA_V), BF16),
            jax.ShapeDtypeStruct((b, s, POOL_WIDTH), BF16),
        ],
        scratch_shapes=[pltpu.VMEM((POOL_HALO, POOL_WIDTH), F32)],
        compiler_params=pltpu.CompilerParams(
            dimension_semantics=("arbitrary", "arbitrary"), vmem_limit_bytes=VMEM_LIMIT),
        name="l0_pre",
    )(x, mod, n1g, w_in_ext, qg, kvg, wq, wkv, rope_tab, pw, pb, ps)


def _l1_pre_kernel(x_ref, mod_ref, n1g_ref, wqkv_ref, tab_ref, q_out, k_out, v_out, *, q_scale):
    t = x_ref.shape[1]
    mod = mod_ref[0]
    h = (_rms(x_ref[0], n1g_ref[...]) * (1.0 + mod[1:2]) + mod[0:1]).astype(BF16)
    lane = lax.broadcasted_iota(jnp.int32, (t, LANES), 1)
    half = ROT_DIFF // 2
    is_x2 = ((lane % DIFF_HD) >= half) & ((lane % DIFF_HD) < ROT_DIFF)
    cos_t, sin_t = _rope_patterns(tab_ref[0], lane, TAB_COS_DIFF, TAB_SIN_DIFF, (0, DIFF_HD), half)
    width = DIFF_HEADS * LANES
    q = jnp.dot(h, wqkv_ref[:, :width], preferred_element_type=F32)
    for hd in range(DIFF_HEADS):
        sl = slice(hd * LANES, (hd + 1) * LANES)
        q_out[0, :, sl] = (_rope_slab(q[:, sl], cos_t, sin_t, is_x2, half) * q_scale).astype(BF16)
    k = jnp.dot(h, wqkv_ref[:, width:2 * width], preferred_element_type=F32)
    for hd in range(DIFF_HEADS):
        sl = slice(hd * LANES, (hd + 1) * LANES)
        k_out[0, :, sl] = _rope_slab(k[:, sl], cos_t, sin_t, is_x2, half).astype(BF16)
    v_out[0] = jnp.dot(h, wqkv_ref[:, 2 * width:], preferred_element_type=F32).astype(BF16)


def _l1_pre(x, mod, n1g, wqkv, rope_tab):
    b, s, d = x.shape
    t = TOK_TILE
    tok = lambda w: pl.BlockSpec((1, t, w), lambda i, j: (i, j, 0))
    width = DIFF_HEADS * LANES
    return pl.pallas_call(
        functools.partial(_l1_pre_kernel, q_scale=DIFF_HD ** -0.5 * LOG2E),
        grid=(b, s // t),
        in_specs=[
            tok(d),
            pl.BlockSpec((1, 6, d), lambda i, j: (i, 0, 0)),
            _const_spec(n1g.shape), _const_spec(wqkv.shape),
            tok(LANES),
        ],
        out_specs=[tok(width)] * 3,
        out_shape=[jax.ShapeDtypeStruct((b, s, width), BF16)] * 3,
        compiler_params=pltpu.CompilerParams(
            dimension_semantics=("arbitrary", "arbitrary"), vmem_limit_bytes=VMEM_LIMIT),
        name="l1_pre",
    )(x, mod, n1g, wqkv, rope_tab)


def _flash_group(qs, k_ats, v_ats, n_full, row_pos, m_sc, acc_sc, p_sc, alpha_sc):
    n_grp = len(qs)
    n_blk = ATTN_TK // LANES
    acc_blk = acc_sc.shape[-1] // LANES
    col = n_full * ATTN_TK + lax.broadcasted_iota(jnp.int32, (qs[0].shape[0], ATTN_TK), 1)
    mask_bias = jnp.where(col <= row_pos, 0.0, MASK_VALUE)

    def scores(g, j, first):
        s = lax.dot_general(qs[g], k_ats[g](j), (((1,), (1,)), ((), ())), preferred_element_type=F32)
        if first:
            s = s + mask_bias
        blocks = [s[:, c * LANES:(c + 1) * LANES] for c in range(n_blk)]
        m_blk = blocks[0]
        for blk in blocks[1:]:
            m_blk = jnp.maximum(m_blk, blk)
        m_new = jnp.max(m_blk, axis=1, keepdims=True)
        if first:
            m_new = jnp.broadcast_to(m_new, m_sc.shape[1:])
        else:
            m_prev = m_sc[g]
            m_new = jnp.maximum(m_prev, m_new)
            alpha_sc[g] = jnp.exp2(m_prev - m_new)
        p_sc[g] = jnp.concatenate([jnp.exp2(blk - m_new).astype(BF16) for blk in blocks], axis=1)
        m_sc[g] = m_new

    def values(g, j):
        pv = jnp.dot(p_sc[g], v_ats[g](j), preferred_element_type=F32)
        alpha = alpha_sc[g]
        alpha_w = alpha if acc_blk == 1 else jnp.concatenate([alpha] * acc_blk, axis=1)
        acc_sc[g] = alpha_w * acc_sc[g] + pv

    for g in range(n_grp):
        scores(g, n_full, True)

    def trip(j):
        prev = jnp.where(j == 0, n_full, j - 1)
        for g in range(n_grp):
            values(g, prev)
            scores(g, j, False)

    def two_trips(t, carry):
        trip(2 * t)
        trip(2 * t + 1)
        return carry

    def odd_trip(_, carry):
        trip(n_full - 1)
        return carry

    lax.fori_loop(0, lax.shift_right_logical(n_full, 1), two_trips, 0)
    lax.fori_loop(0, n_full & 1, odd_trip, 0)
    last = jnp.maximum(n_full - 1, 0)
    for g in range(n_grp):
        values(g, last)


def _flash_scratch(rows, acc_width):
    return [pltpu.VMEM((ATTN_GROUP, rows, LANES), F32), pltpu.VMEM((ATTN_GROUP, rows, acc_width), F32),
            pltpu.VMEM((ATTN_GROUP, rows, ATTN_TK), BF16), pltpu.VMEM((ATTN_GROUP, rows, LANES), F32)]


def _zero_at_first_step(acc_sc, alpha_sc):
    @pl.when((pl.program_id(0) == 0) & (pl.program_id(1) == 0))
    def _():
        def zero(g, carry):
            acc_sc[g] = jnp.zeros(acc_sc.shape[1:], F32)
            alpha_sc[g] = jnp.zeros(alpha_sc.shape[1:], F32)
            return carry

        lax.fori_loop(0, acc_sc.shape[0], zero, 0)


def _key_step(ref, lanes):
    return lambda j: ref[0, pl.ds(pl.multiple_of(j * ATTN_TK, ATTN_TK), ATTN_TK), lanes]


def _mla_attn_kernel(q_ref, k_ref, v_ref, o_ref, m_sc, acc_sc, p_sc, alpha_sc):
    tq = q_ref.shape[1]
    qi = pl.program_id(1)
    n_full = (qi * tq) // ATTN_TK
    row_pos = qi * tq + lax.broadcasted_iota(jnp.int32, (tq, 1), 0)
    lane = lax.broadcasted_iota(jnp.int32, (tq, LANES), 1)
    one = jnp.ones((ATTN_TK, LANES), BF16)
    n_grp = m_sc.shape[0]
    _zero_at_first_step(acc_sc, alpha_sc)

    def value_step(hd):
        raw = _key_step(v_ref, slice((hd // 2) * LANES, (hd // 2 + 1) * LANES))
        return lambda j: jnp.concatenate([raw(j), one], axis=1)

    for first in range(0, MLA_HEADS, n_grp):
        heads = range(first, first + n_grp)
        slabs = [slice(hd * LANES, (hd + 1) * LANES) for hd in heads]
        _flash_group([q_ref[0, :, sl] for sl in slabs], [_key_step(k_ref, sl) for sl in slabs],
                     [value_step(hd) for hd in heads], n_full, row_pos, m_sc, acc_sc, p_sc, alpha_sc)
        for g in range(0, n_grp, 2):
            even = acc_sc[g, :, :LANES] / acc_sc[g, :, LANES:]
            odd = acc_sc[g + 1, :, :LANES] / acc_sc[g + 1, :, LANES:]
            pair = (first + g) // 2
            o_ref[0, :, pair * LANES:(pair + 1) * LANES] = jnp.where(lane < MLA_V, even, odd).astype(BF16)
            acc_sc[g] = jnp.zeros(acc_sc.shape[1:], F32)
            acc_sc[g + 1] = jnp.zeros(acc_sc.shape[1:], F32)


def _mla_attention(q, k, v):
    b, s, _ = q.shape
    tq = MLA_TQ
    return pl.pallas_call(
        _mla_attn_kernel,
        grid=(b, s // tq),
        in_specs=[
            pl.BlockSpec((1, tq, q.shape[2]), lambda i, j: (i, j, 0)),
            pl.BlockSpec((1, s, k.shape[2]), lambda i, j: (i, 0, 0)),
            pl.BlockSpec((1, s, v.shape[2]), lambda i, j: (i, 0, 0)),
        ],
        out_specs=pl.BlockSpec((1, tq, v.shape[2]), lambda i, j: (i, j, 0)),
        out_shape=jax.ShapeDtypeStruct(v.shape, BF16),
        scratch_shapes=_flash_scratch(tq, 2 * LANES),
        compiler_params=pltpu.CompilerParams(
            dimension_semantics=("arbitrary", "arbitrary"), vmem_limit_bytes=VMEM_LIMIT),
        name="mla_attention",
    )(q, k, v)


def _diff_attn_kernel(q_ref, k_ref, v_ref, lam_ref, g_ref, o_ref, m_sc, acc_sc, p_sc, alpha_sc, *, lambda_init):
    tq = q_ref.shape[1]
    qi = pl.program_id(1)
    n_full = (qi * tq) // ATTN_TK
    row = lax.broadcasted_iota(jnp.int32, (2 * tq, 1), 0)
    row_pos = qi * tq + jnp.where(row >= tq, row - tq, row)
    lane = lax.broadcasted_iota(jnp.int32, (tq, LANES), 1)
    one = jnp.ones((ATTN_TK, LANES), BF16)
    _zero_at_first_step(acc_sc, alpha_sc)
    lv = lam_ref[...]
    lam = (jnp.exp(jnp.sum(lv[0:1] * lv[1:2], axis=1, keepdims=True))
           - jnp.exp(jnp.sum(lv[2:3] * lv[3:4], axis=1, keepdims=True)) + lambda_init)
    n_grp = m_sc.shape[0]
    for first in range(0, DIFF_HEADS, n_grp):
        slabs = [slice(hd * LANES, (hd + 1) * LANES) for hd in range(first, first + n_grp)]
        qs = []
        for sl in slabs:
            qh = q_ref[0, :, sl]
            zero = jnp.zeros_like(qh)
            qs.append(jnp.concatenate(
                [jnp.where(lane < DIFF_HD, qh, zero), jnp.where(lane >= DIFF_HD, qh, zero)], axis=0))
        v_ats = [lambda j, raw=_key_step(v_ref, sl): jnp.concatenate([raw(j), one], axis=1) for sl in slabs]
        _flash_group(qs, [_key_step(k_ref, sl) for sl in slabs], v_ats, n_full, row_pos,
                     m_sc, acc_sc, p_sc, alpha_sc)
        for g, sl in enumerate(slabs):
            o = acc_sc[g, :, :LANES] / acc_sc[g, :, LANES:]
            o = o[:tq] - lam * o[tq:]
            o_ref[0, :, sl] = (_rms(o, g_ref[...]) * (1.0 - lambda_init)).astype(BF16)
            acc_sc[g] = jnp.zeros(acc_sc.shape[1:], F32)


def _diff_attention(q, k, v, lam_vecs, subln_g, lambda_init):
    b, s, w = q.shape
    tq = DIFF_TQ
    return pl.pallas_call(
        functools.partial(_diff_attn_kernel, lambda_init=lambda_init),
        grid=(b, s // tq),
        in_specs=[
            pl.BlockSpec((1, tq, w), lambda i, j: (i, j, 0)),
            pl.BlockSpec((1, s, w), lambda i, j: (i, 0, 0)),
            pl.BlockSpec((1, s, w), lambda i, j: (i, 0, 0)),
            _const_spec(lam_vecs.shape), _const_spec(subln_g.shape),
        ],
        out_specs=pl.BlockSpec((1, tq, w), lambda i, j: (i, j, 0)),
        out_shape=jax.ShapeDtypeStruct((b, s, w), BF16),
        scratch_shapes=_flash_scratch(2 * tq, 2 * LANES),
        compiler_params=pltpu.CompilerParams(
            dimension_semantics=("arbitrary", "arbitrary"), vmem_limit_bytes=VMEM_LIMIT),
        name="diff_attention",
    )(q, k, v, lam_vecs, subln_g)


def _post_kernel(*refs, n_mix, final_norm):
    mix_refs = refs[:n_mix]
    x_ref, mod_ref, wout_ref, n2g_ref, wgu_ref, wd_ref = refs[n_mix:n_mix + 6]
    rest = refs[n_mix + 6:]
    if final_norm:
        fng_ref, o_ref, acc_sc = rest
    else:
        o_ref, acc_sc = rest
    mod = mod_ref[0]
    y = None
    row = 0
    for r in mix_refs:
        w = r.shape[2]
        part = jnp.dot(r[0], wout_ref[row:row + w, :], preferred_element_type=F32)
        y = part if y is None else y + part
        row += w
    x1 = x_ref[0] + mod[2:3] * y
    h = (_rms(x1, n2g_ref[...]) * (1.0 + mod[4:5]) + mod[3:4]).astype(BF16)
    acc_sc[...] = jnp.zeros_like(acc_sc)

    for c in range(N_FF_CHUNKS):
        cols = slice(c * FF_CHUNK, (c + 1) * FF_CHUNK)
        ucols = slice(D_FF + c * FF_CHUNK, D_FF + (c + 1) * FF_CHUNK)
        g = jnp.dot(h, wgu_ref[:, cols], preferred_element_type=F32)
        u = jnp.dot(h, wgu_ref[:, ucols], preferred_element_type=F32)
        act = (g * jax.nn.sigmoid(g) * u).astype(BF16)
        acc_sc[...] += jnp.dot(act, wd_ref[cols, :], preferred_element_type=F32)
    x2 = x1 + mod[5:6] * acc_sc[...]
    if final_norm:
        x2 = _rms(x2, fng_ref[...])
    o_ref[0] = x2


def _post(mix_inputs, x, mod, wout, n2g, wgu, wd, final_g=None):
    b, s, d = x.shape
    t = POST_TILE
    tok = lambda w: pl.BlockSpec((1, t, w), lambda i, j: (i, j, 0))
    final_norm = final_g is not None
    in_specs = [tok(m.shape[2]) for m in mix_inputs] + [
        tok(d),
        pl.BlockSpec((1, 6, d), lambda i, j: (i, 0, 0)),
        _const_spec(wout.shape), _const_spec(n2g.shape), _const_spec(wgu.shape), _const_spec(wd.shape),
    ]
    args = list(mix_inputs) + [x, mod, wout, n2g, wgu, wd]
    if final_norm:
        in_specs.append(_const_spec(final_g.shape))
        args.append(final_g)
    return pl.pallas_call(
        functools.partial(_post_kernel, n_mix=len(mix_inputs), final_norm=final_norm),
        grid=(b, s // t),
        in_specs=in_specs,
        out_specs=tok(d),
        out_shape=jax.ShapeDtypeStruct((b, s, d), F32),
        scratch_shapes=[pltpu.VMEM((t, d), F32)],
        compiler_params=pltpu.CompilerParams(
            dimension_semantics=("arbitrary", "arbitrary"), vmem_limit_bytes=VMEM_LIMIT),
        name="post_final" if final_norm else "post",
    )(*args)


def _ffn_weights(w_gate_up, w_down):
    return w_gate_up.astype(BF16), w_down.astype(BF16)


def kernel(x, c, positions, ada_w, ada_b, norm1_g, norm2_g, ffn_w_gate_up, ffn_w_down, mla_w_in, mla_q_norm_g,
           mla_kv_norm_g, mla_w_uq, mla_w_ukv, pool_w, pool_b, pool_scale, mix_a_w_out, diff_w_qkv,
           diff_lambda_q1, diff_lambda_k1, diff_lambda_q2, diff_lambda_k2, diff_subln_g, diff_w_out,
           final_norm_g):
    d = D_MODEL
    rope_tab = _rope_table(positions)
    mod = _modulation(c, ada_w, ada_b)

    w_in = mla_w_in[0]
    o_kv, o_kr, o_u = Q_LORA, Q_LORA + KV_LORA, Q_LORA + KV_LORA + MLA_ROPE
    w_kr = jnp.pad(w_in[:, o_kr:o_u], ((0, 0), (MLA_NOPE, LANES - MLA_NOPE - MLA_ROPE)))
    w_in_ext = jnp.concatenate([w_in[:, :o_kr], w_kr, w_in[:, o_u:]], axis=1).astype(BF16)
    qk_dim = MLA_NOPE + MLA_ROPE
    wq = jnp.pad(mla_w_uq[0].reshape(Q_LORA, MLA_HEADS, qk_dim), ((0, 0), (0, 0), (0, LANES - qk_dim)))
    wq = wq.reshape(Q_LORA, MLA_HEADS * LANES).astype(BF16)
    w_ukv = mla_w_ukv[0].reshape(KV_LORA, MLA_HEADS, MLA_NOPE + MLA_V)
    wk = jnp.pad(w_ukv[..., :MLA_NOPE], ((0, 0), (0, 0), (0, LANES - MLA_NOPE))).reshape(KV_LORA, MLA_HEADS * LANES)
    wv = w_ukv[..., MLA_NOPE:].reshape(KV_LORA, MLA_HEADS * MLA_V)
    wkv = jnp.concatenate([wk, wv], axis=1).astype(BF16)
    q0, k0, v0, pool = _l0_pre(
        x, mod[0], norm1_g[0:1], w_in_ext, mla_q_norm_g, mla_kv_norm_g, wq, wkv, rope_tab,
        pool_w[0].astype(BF16), pool_b[0].reshape(1, POOL_WIDTH), pool_scale)
    attn0 = _mla_attention(q0, k0, v0)
    wgu0, wd0 = _ffn_weights(ffn_w_gate_up[0], ffn_w_down[0])
    x = _post([attn0, pool], x, mod[0], mix_a_w_out[0].astype(BF16), norm2_g[0:1], wgu0, wd0)

    lambda_init = 0.8 - 0.6 * math.exp(-0.3 * 1)
    q1, k1, v1 = _l1_pre(x, mod[1], norm1_g[1:2], diff_w_qkv[0].astype(BF16), rope_tab)
    lam_vecs = jnp.concatenate([diff_lambda_q1, diff_lambda_k1, diff_lambda_q2, diff_lambda_k2], axis=0)
    attn1 = _diff_attention(q1, k1, v1, lam_vecs, diff_subln_g, lambda_init)
    wgu1, wd1 = _ffn_weights(ffn_w_gate_up[1], ffn_w_down[1])
    return _post([attn1], x, mod[1], diff_w_out[0].astype(BF16), norm2_g[1:2], wgu1, wd1,
                 final_g=final_norm_g.reshape(1, d))
```

```python
import functools
import math

import jax
import jax.numpy as jnp
from jax import lax
from jax.experimental import pallas as pl
from jax.experimental.pallas import tpu as pltpu

D_MODEL = 1024
EPS = 1e-6
ROPE_THETA = 500000.0
MLA_HEADS = 8
MLA_NOPE = 64
MLA_ROPE = 32
MLA_V = 64
Q_LORA = 384
KV_LORA = 256
POOL_WINDOWS = (2, 4, 8, 16)
POOL_GROUP = 128
POOL_WIDTH = POOL_GROUP * len(POOL_WINDOWS)
DIFF_HEADS = 8
DIFF_HD = 64
ROT_DIFF = DIFF_HD // 4
D_FF = 2816

LANES = 128
POOL_HALO = 16
FF_CHUNK = 256
N_FF_CHUNKS = D_FF // FF_CHUNK
assert N_FF_CHUNKS * FF_CHUNK == D_FF

TOK_TILE = 1024
POST_TILE = 1024
MLA_TQ = 512
DIFF_TQ = 256
ATTN_TK = 512
ATTN_GROUP = 8
VMEM_LIMIT = 56 * 1024 * 1024

LOG2E = 1.4426950408889634
MASK_VALUE = -1e30

F32 = jnp.float32
BF16 = jnp.bfloat16


def _const_spec(shape):
    nd = len(shape)
    return pl.BlockSpec(shape, lambda *_: (0,) * nd, pipeline_mode=pl.Buffered(1))


def _rms(x, g):
    return x * lax.rsqrt(jnp.mean(x * x, axis=-1, keepdims=True) + EPS) * g


ROPE_PAIR = LANES // 2
MLA_X1_LANE = ROPE_PAIR - MLA_ROPE // 2


def _rope_slab(x, cos_t, sin_t):
    return x * cos_t + pltpu.roll(x, ROPE_PAIR, 1) * sin_t


TAB_FREQS = 32
TAB_COS_MLA, TAB_COS_DIFF = 0, 16
TAB_SIN_MLA, TAB_SIN_DIFF = TAB_FREQS, TAB_FREQS + 16
TAB_SLOTS = LANES // TAB_FREQS


def _rope_table_kernel(pos_ref, inv_ref, tab_ref):
    ang = pos_ref[...].astype(F32) * inv_ref[...]
    cos = jnp.cos(ang)
    sin = jnp.sin(ang)
    lane = lax.broadcasted_iota(jnp.int32, ang.shape, 1)
    for slot in range(TAB_SLOTS):
        shift_c = (LANES - slot * TAB_FREQS) % LANES
        shift_s = (LANES + TAB_FREQS - slot * TAB_FREQS) % LANES
        c = cos if shift_c == 0 else pltpu.roll(cos, shift_c, 1)
        s = sin if shift_s == 0 else pltpu.roll(sin, shift_s, 1)
        tab_ref[slot] = jnp.where(lane < TAB_FREQS, c, jnp.where(lane < 2 * TAB_FREQS, s, 0.0))


def _rope_table(positions):
    b, s = positions.shape
    rows = b * s // TAB_SLOTS
    inv_mla = ROPE_THETA ** (-jnp.arange(0, MLA_ROPE, 2, dtype=F32) / MLA_ROPE)
    inv_diff = ROPE_THETA ** (-jnp.arange(0, ROT_DIFF, 2, dtype=F32) / ROT_DIFF)
    inv = jnp.concatenate([inv_mla, inv_diff, inv_diff])
    inv = jnp.tile(inv, TAB_SLOTS).reshape(1, LANES)
    pos = jnp.broadcast_to(positions.reshape(TAB_SLOTS, rows).T[:, :, None], (rows, TAB_SLOTS, TAB_FREQS))
    pos = pos.reshape(rows, LANES)
    blk = 1024
    tab = pl.pallas_call(
        _rope_table_kernel,
        grid=(rows // blk,),
        in_specs=[pl.BlockSpec((blk, LANES), lambda i: (i, 0)), _const_spec((1, LANES))],
        out_specs=pl.BlockSpec((TAB_SLOTS, blk, LANES), lambda i: (0, i, 0)),
        out_shape=jax.ShapeDtypeStruct((TAB_SLOTS, rows, LANES), F32),
        name="rope_table",
    )(pos, inv)
    return tab.reshape(b, s, LANES)


def _lane_pattern(tab, pieces, default):
    lane = lax.broadcasted_iota(jnp.int32, (1, LANES), 1)
    out = jnp.full_like(tab, default)
    for lo, hi, src, sign in pieces:
        rolled = pltpu.roll(tab, (lo - src) % LANES, 1)
        out = jnp.where((lane >= lo) & (lane < hi), -rolled if sign < 0 else rolled, out)
    return out


def _rope_patterns(tab, cos_src, sin_src, first, width):
    lo2 = first + ROPE_PAIR
    cos_t = _lane_pattern(tab, [(first, first + width, cos_src, 1), (lo2, lo2 + width, cos_src, 1)], 1.0)
    sin_t = _lane_pattern(tab, [(first, first + width, sin_src, -1), (lo2, lo2 + width, sin_src, 1)], 0.0)
    return cos_t, sin_t


def _mod_kernel(c_ref, w_ref, b_ref, o_ref):
    c = c_ref[...]
    cond = (c * jax.nn.sigmoid(c)).astype(BF16)
    o_ref[0] = jnp.dot(cond, w_ref[0].astype(BF16), preferred_element_type=F32) + b_ref[0]


def _modulation(c, ada_w, ada_b):
    depth, d, n = ada_w.shape
    b = c.shape[0]
    blk = 1536
    mod = pl.pallas_call(
        _mod_kernel,
        grid=(depth, n // blk),
        in_specs=[
            _const_spec((b, d)),
            pl.BlockSpec((1, d, blk), lambda i, j: (i, 0, j)),
            pl.BlockSpec((1, 1, blk), lambda i, j: (i, 0, j)),
        ],
        out_specs=pl.BlockSpec((1, b, blk), lambda i, j: (i, 0, j)),
        out_shape=jax.ShapeDtypeStruct((depth, b, n), F32),
        compiler_params=pltpu.CompilerParams(vmem_limit_bytes=VMEM_LIMIT),
        name="adaln_mod",
    )(c, ada_w, ada_b.reshape(depth, 1, n))
    return mod.reshape(depth, b, 6, d)


def _l0_pre_kernel(x_ref, mod_ref, n1g_ref, win_ref, qg_ref, kvg_ref, wq_ref, wkv_ref, tab_ref,
                   pw_ref, pb_ref, ps_ref, q_out, k_out, v_out, pool_out, halo_sc, *, q_scale):
    si = pl.program_id(1)
    t = x_ref.shape[1]

    @pl.when(si == 0)
    def _():
        halo_sc[...] = jnp.zeros_like(halo_sc)

    mod = mod_ref[0]
    h = _rms(x_ref[0], n1g_ref[...]) * (1.0 + mod[1:2]) + mod[0:1]
    h = h.astype(BF16)
    n_lat = Q_LORA + KV_LORA + LANES
    proj = jnp.dot(h, win_ref[:, :n_lat], preferred_element_type=F32)
    u = jnp.dot(h, win_ref[:, n_lat:], preferred_element_type=F32)
    c_q = proj[:, :Q_LORA]
    c_kv = proj[:, Q_LORA:Q_LORA + KV_LORA]
    k_rope = proj[:, Q_LORA + KV_LORA:]

    cos_t, sin_t = _rope_patterns(tab_ref[0], TAB_COS_MLA, TAB_SIN_MLA, MLA_X1_LANE, MLA_ROPE // 2)

    q = jnp.dot(_rms(c_q, qg_ref[...]).astype(BF16), wq_ref[...], preferred_element_type=F32)
    for hd in range(MLA_HEADS):
        sl = slice(hd * LANES, (hd + 1) * LANES)
        q_out[0, :, sl] = (_rope_slab(q[:, sl], cos_t, sin_t) * q_scale).astype(BF16)

    kv = jnp.dot(_rms(c_kv, kvg_ref[...]).astype(BF16), wkv_ref[...], preferred_element_type=F32)
    k_rope = _rope_slab(k_rope, cos_t, sin_t)
    for hd in range(MLA_HEADS):
        sl = slice(hd * LANES, (hd + 1) * LANES)
        k_out[0, :, sl] = (kv[:, sl] + k_rope).astype(BF16)
    v_out[0] = kv[:, MLA_HEADS * LANES:].astype(BF16)

    ext = jnp.concatenate([halo_sc[...], u], axis=0)
    halo_sc[...] = u[t - POOL_HALO:, :]
    tok = si * t + lax.broadcasted_iota(jnp.int32, (t, 1), 0)
    for g, win in enumerate(POOL_WINDOWS):
        sl = slice(g * POOL_GROUP, (g + 1) * POOL_GROUP)
        e = ext[:, sl]
        shift = 1
        while shift < win:
            e = e + pltpu.roll(e, shift, 0)
            shift *= 2
        cnt = jnp.minimum(tok + 1, win).astype(F32)
        pooled = e[POOL_HALO:] * (1.0 / cnt) - u[:, sl]
        y = jnp.dot(pooled.astype(BF16), pw_ref[g], preferred_element_type=F32) + pb_ref[:, sl]
        pool_out[0, :, sl] = (y * ps_ref[:, sl]).astype(BF16)


def _l0_pre(x, mod, n1g, w_in_ext, qg, kvg, wq, wkv, rope_tab, pw, pb, ps):
    b, s, d = x.shape
    t = TOK_TILE
    tok = lambda w: pl.BlockSpec((1, t, w), lambda i, j: (i, j, 0))
    q_scale = (MLA_NOPE + MLA_ROPE) ** -0.5 * LOG2E
    return pl.pallas_call(
        functools.partial(_l0_pre_kernel, q_scale=q_scale),
        grid=(b, s // t),
        in_specs=[
            tok(d),
            pl.BlockSpec((1, 6, d), lambda i, j: (i, 0, 0)),
            _const_spec(n1g.shape), _const_spec(w_in_ext.shape), _const_spec(qg.shape), _const_spec(kvg.shape),
            _const_spec(wq.shape), _const_spec(wkv.shape),
            tok(LANES),
            _const_spec(pw.shape), _const_spec(pb.shape), _const_spec(ps.shape),
        ],
        out_specs=[tok(MLA_HEADS * LANES), tok(MLA_HEADS * LANES), tok(MLA_HEADS * MLA_V), tok(POOL_WIDTH)],
        out_shape=[
            jax.ShapeDtypeStruct((b, s, MLA_HEADS * LANES), BF16),
            jax.ShapeDtypeStruct((b, s, MLA_HEADS * LANES), BF16),
            jax.ShapeDtypeStruct((b, s, MLA_HEADS * MLA_V), BF16),
            jax.ShapeDtypeStruct((b, s, POOL_WIDTH), BF16),
        ],
        scratch_shapes=[pltpu.VMEM((POOL_HALO, POOL_WIDTH), F32)],
        compiler_params=pltpu.CompilerParams(
            dimension_semantics=("arbitrary", "arbitrary"), vmem_limit_bytes=VMEM_LIMIT),
        name="l0_pre",
    )(x, mod, n1g, w_in_ext, qg, kvg, wq, wkv, rope_tab, pw, pb, ps)


def _l1_pre_kernel(x_ref, mod_ref, n1g_ref, wqkv_ref, tab_ref, q_out, k_out, v_out, *, q_scale):
    t = x_ref.shape[1]
    mod = mod_ref[0]
    h = (_rms(x_ref[0], n1g_ref[...]) * (1.0 + mod[1:2]) + mod[0:1]).astype(BF16)
    cos_t, sin_t = _rope_patterns(tab_ref[0], TAB_COS_DIFF, TAB_SIN_DIFF, 0, ROT_DIFF)
    width = DIFF_HEADS * LANES
    q = jnp.dot(h, wqkv_ref[:, :width], preferred_element_type=F32)
    for hd in range(DIFF_HEADS):
        sl = slice(hd * LANES, (hd + 1) * LANES)
        q_out[0, :, sl] = (_rope_slab(q[:, sl], cos_t, sin_t) * q_scale).astype(BF16)
    k = jnp.dot(h, wqkv_ref[:, width:2 * width], preferred_element_type=F32)
    for hd in range(DIFF_HEADS):
        sl = slice(hd * LANES, (hd + 1) * LANES)
        k_out[0, :, sl] = _rope_slab(k[:, sl], cos_t, sin_t).astype(BF16)
    v_out[0] = jnp.dot(h, wqkv_ref[:, 2 * width:], preferred_element_type=F32).astype(BF16)


def _l1_pre(x, mod, n1g, wqkv, rope_tab):
    b, s, d = x.shape
    t = TOK_TILE
    tok = lambda w: pl.BlockSpec((1, t, w), lambda i, j: (i, j, 0))
    width = DIFF_HEADS * LANES
    return pl.pallas_call(
        functools.partial(_l1_pre_kernel, q_scale=DIFF_HD ** -0.5 * LOG2E),
        grid=(b, s // t),
        in_specs=[
            tok(d),
            pl.BlockSpec((1, 6, d), lambda i, j: (i, 0, 0)),
            _const_spec(n1g.shape), _const_spec(wqkv.shape),
            tok(LANES),
        ],
        out_specs=[tok(width)] * 3,
        out_shape=[jax.ShapeDtypeStruct((b, s, width), BF16)] * 3,
        compiler_params=pltpu.CompilerParams(
            dimension_semantics=("arbitrary", "arbitrary"), vmem_limit_bytes=VMEM_LIMIT),
        name="l1_pre",
    )(x, mod, n1g, wqkv, rope_tab)


def _flash_group(qs, k_ats, v_ats, n_full, row_pos, m_sc, acc_sc, p_sc, alpha_sc):
    n_grp = len(qs)
    n_blk = ATTN_TK // LANES
    acc_blk = acc_sc.shape[-1] // LANES
    col = n_full * ATTN_TK + lax.broadcasted_iota(jnp.int32, (qs[0].shape[0], ATTN_TK), 1)
    mask_bias = jnp.where(col <= row_pos, 0.0, MASK_VALUE)

    def scores(g, j, first):
        s = lax.dot_general(qs[g], k_ats[g](j), (((1,), (1,)), ((), ())), preferred_element_type=F32)
        if first:
            s = s + mask_bias
        blocks = [s[:, c * LANES:(c + 1) * LANES] for c in range(n_blk)]
        m_blk = blocks[0]
        for blk in blocks[1:]:
            m_blk = jnp.maximum(m_blk, blk)
        m_new = jnp.max(m_blk, axis=1, keepdims=True)
        if first:
            m_new = jnp.broadcast_to(m_new, m_sc.shape[1:])
        else:
            m_prev = m_sc[g]
            m_new = jnp.maximum(m_prev, m_new)
            alpha_sc[g] = jnp.exp2(m_prev - m_new)
        p_sc[g] = jnp.concatenate([jnp.exp2(blk - m_new).astype(BF16) for blk in blocks], axis=1)
        m_sc[g] = m_new

    def values(g, j):
        pv = jnp.dot(p_sc[g], v_ats[g](j), preferred_element_type=F32)
        alpha = alpha_sc[g]
        alpha_w = alpha if acc_blk == 1 else jnp.concatenate([alpha] * acc_blk, axis=1)
        acc_sc[g] = alpha_w * acc_sc[g] + pv

    for g in range(n_grp):
        scores(g, n_full, True)

    def trip(j):
        prev = jnp.where(j == 0, n_full, j - 1)
        for g in range(n_grp):
            values(g, prev)
            scores(g, j, False)

    def two_trips(t, carry):
        trip(2 * t)
        trip(2 * t + 1)
        return carry

    def odd_trip(_, carry):
        trip(n_full - 1)
        return carry

    lax.fori_loop(0, lax.shift_right_logical(n_full, 1), two_trips, 0)
    lax.fori_loop(0, n_full & 1, odd_trip, 0)
    last = jnp.maximum(n_full - 1, 0)
    for g in range(n_grp):
        values(g, last)


def _flash_scratch(rows, acc_width):
    return [pltpu.VMEM((ATTN_GROUP, rows, LANES), F32), pltpu.VMEM((ATTN_GROUP, rows, acc_width), F32),
            pltpu.VMEM((ATTN_GROUP, rows, ATTN_TK), BF16), pltpu.VMEM((ATTN_GROUP, rows, LANES), F32)]


def _zero_at_first_step(acc_sc, alpha_sc):
    @pl.when((pl.program_id(0) == 0) & (pl.program_id(1) == 0))
    def _():
        def zero(g, carry):
            acc_sc[g] = jnp.zeros(acc_sc.shape[1:], F32)
            alpha_sc[g] = jnp.zeros(alpha_sc.shape[1:], F32)
            return carry

        lax.fori_loop(0, acc_sc.shape[0], zero, 0)


def _key_step(ref, lanes):
    return lambda j: ref[0, pl.ds(pl.multiple_of(j * ATTN_TK, ATTN_TK), ATTN_TK), lanes]


def _mla_attn_kernel(q_ref, k_ref, v_ref, o_ref, m_sc, acc_sc, p_sc, alpha_sc):
    tq = q_ref.shape[1]
    qi = pl.program_id(1)
    n_full = (qi * tq) // ATTN_TK
    row_pos = qi * tq + lax.broadcasted_iota(jnp.int32, (tq, 1), 0)
    lane = lax.broadcasted_iota(jnp.int32, (tq, LANES), 1)
    one = jnp.ones((ATTN_TK, LANES), BF16)
    n_grp = m_sc.shape[0]
    _zero_at_first_step(acc_sc, alpha_sc)

    def value_step(hd):
        raw = _key_step(v_ref, slice((hd // 2) * LANES, (hd // 2 + 1) * LANES))
        return lambda j: jnp.concatenate([raw(j), one], axis=1)

    for first in range(0, MLA_HEADS, n_grp):
        heads = range(first, first + n_grp)
        slabs = [slice(hd * LANES, (hd + 1) * LANES) for hd in heads]
        _flash_group([q_ref[0, :, sl] for sl in slabs], [_key_step(k_ref, sl) for sl in slabs],
                     [value_step(hd) for hd in heads], n_full, row_pos, m_sc, acc_sc, p_sc, alpha_sc)
        for g in range(0, n_grp, 2):
            even = acc_sc[g, :, :LANES] / acc_sc[g, :, LANES:]
            odd = acc_sc[g + 1, :, :LANES] / acc_sc[g + 1, :, LANES:]
            pair = (first + g) // 2
            o_ref[0, :, pair * LANES:(pair + 1) * LANES] = jnp.where(lane < MLA_V, even, odd).astype(BF16)
            acc_sc[g] = jnp.zeros(acc_sc.shape[1:], F32)
            acc_sc[g + 1] = jnp.zeros(acc_sc.shape[1:], F32)


def _mla_attention(q, k, v):
    b, s, _ = q.shape
    tq = MLA_TQ
    return pl.pallas_call(
        _mla_attn_kernel,
        grid=(b, s // tq),
        in_specs=[
            pl.BlockSpec((1, tq, q.shape[2]), lambda i, j: (i, j, 0)),
            pl.BlockSpec((1, s, k.shape[2]), lambda i, j: (i, 0, 0)),
            pl.BlockSpec((1, s, v.shape[2]), lambda i, j: (i, 0, 0)),
        ],
        out_specs=pl.BlockSpec((1, tq, v.shape[2]), lambda i, j: (i, j, 0)),
        out_shape=jax.ShapeDtypeStruct(v.shape, BF16),
        scratch_shapes=_flash_scratch(tq, 2 * LANES),
        compiler_params=pltpu.CompilerParams(
            dimension_semantics=("arbitrary", "arbitrary"), vmem_limit_bytes=VMEM_LIMIT),
        name="mla_attention",
    )(q, k, v)


def _diff_attn_kernel(q_ref, k_ref, v_ref, lam_ref, g_ref, o_ref, m_sc, acc_sc, p_sc, alpha_sc, *, lambda_init):
    tq = q_ref.shape[1]
    qi = pl.program_id(1)
    n_full = (qi * tq) // ATTN_TK
    row = lax.broadcasted_iota(jnp.int32, (2 * tq, 1), 0)
    row_pos = qi * tq + jnp.where(row >= tq, row - tq, row)
    lane = lax.broadcasted_iota(jnp.int32, (tq, LANES), 1)
    half = ROT_DIFF // 2
    comp0 = (lane < half) | ((lane >= ROT_DIFF) & (lane < DIFF_HD + half))
    one = jnp.ones((ATTN_TK, LANES), BF16)
    _zero_at_first_step(acc_sc, alpha_sc)
    lv = lam_ref[...]
    lam = (jnp.exp(jnp.sum(lv[0:1] * lv[1:2], axis=1, keepdims=True))
           - jnp.exp(jnp.sum(lv[2:3] * lv[3:4], axis=1, keepdims=True)) + lambda_init)
    n_grp = m_sc.shape[0]
    for first in range(0, DIFF_HEADS, n_grp):
        slabs = [slice(hd * LANES, (hd + 1) * LANES) for hd in range(first, first + n_grp)]
        qs = []
        for sl in slabs:
            qh = q_ref[0, :, sl]
            zero = jnp.zeros_like(qh)
            qs.append(jnp.concatenate(
                [jnp.where(comp0, qh, zero), jnp.where(comp0, zero, qh)], axis=0))
        v_ats = [lambda j, raw=_key_step(v_ref, sl): jnp.concatenate([raw(j), one], axis=1) for sl in slabs]
        _flash_group(qs, [_key_step(k_ref, sl) for sl in slabs], v_ats, n_full, row_pos,
                     m_sc, acc_sc, p_sc, alpha_sc)
        for g, sl in enumerate(slabs):
            o = acc_sc[g, :, :LANES] / acc_sc[g, :, LANES:]
            o = o[:tq] - lam * o[tq:]
            o_ref[0, :, sl] = (_rms(o, g_ref[...]) * (1.0 - lambda_init)).astype(BF16)
            acc_sc[g] = jnp.zeros(acc_sc.shape[1:], F32)


def _diff_attention(q, k, v, lam_vecs, subln_g, lambda_init):
    b, s, w = q.shape
    tq = DIFF_TQ
    return pl.pallas_call(
        functools.partial(_diff_attn_kernel, lambda_init=lambda_init),
        grid=(b, s // tq),
        in_specs=[
            pl.BlockSpec((1, tq, w), lambda i, j: (i, j, 0)),
            pl.BlockSpec((1, s, w), lambda i, j: (i, 0, 0)),
            pl.BlockSpec((1, s, w), lambda i, j: (i, 0, 0)),
            _const_spec(lam_vecs.shape), _const_spec(subln_g.shape),
        ],
        out_specs=pl.BlockSpec((1, tq, w), lambda i, j: (i, j, 0)),
        out_shape=jax.ShapeDtypeStruct((b, s, w), BF16),
        scratch_shapes=_flash_scratch(2 * tq, 2 * LANES),
        compiler_params=pltpu.CompilerParams(
            dimension_semantics=("arbitrary", "arbitrary"), vmem_limit_bytes=VMEM_LIMIT),
        name="diff_attention",
    )(q, k, v, lam_vecs, subln_g)


def _post_kernel(*refs, n_mix, final_norm):
    mix_refs = refs[:n_mix]
    x_ref, mod_ref, wout_ref, n2g_ref, wgu_ref, wd_ref = refs[n_mix:n_mix + 6]
    rest = refs[n_mix + 6:]
    if final_norm:
        fng_ref, o_ref, acc_sc = rest
    else:
        o_ref, acc_sc = rest
    mod = mod_ref[0]
    y = None
    row = 0
    for r in mix_refs:
        w = r.shape[2]
        part = jnp.dot(r[0], wout_ref[row:row + w, :], preferred_element_type=F32)
        y = part if y is None else y + part
        row += w
    x1 = x_ref[0] + mod[2:3] * y
    h = (_rms(x1, n2g_ref[...]) * (1.0 + mod[4:5]) + mod[3:4]).astype(BF16)
    acc_sc[...] = jnp.zeros_like(acc_sc)

    for c in range(N_FF_CHUNKS):
        cols = slice(c * FF_CHUNK, (c + 1) * FF_CHUNK)
        ucols = slice(D_FF + c * FF_CHUNK, D_FF + (c + 1) * FF_CHUNK)
        g = jnp.dot(h, wgu_ref[:, cols], preferred_element_type=F32)
        u = jnp.dot(h, wgu_ref[:, ucols], preferred_element_type=F32)
        act = (g * jax.nn.sigmoid(g) * u).astype(BF16)
        acc_sc[...] += jnp.dot(act, wd_ref[cols, :], preferred_element_type=F32)
    x2 = x1 + mod[5:6] * acc_sc[...]
    if final_norm:
        x2 = _rms(x2, fng_ref[...])
    o_ref[0] = x2


def _post(mix_inputs, x, mod, wout, n2g, wgu, wd, final_g=None):
    b, s, d = x.shape
    t = POST_TILE
    tok = lambda w: pl.BlockSpec((1, t, w), lambda i, j: (i, j, 0))
    final_norm = final_g is not None
    in_specs = [tok(m.shape[2]) for m in mix_inputs] + [
        tok(d),
        pl.BlockSpec((1, 6, d), lambda i, j: (i, 0, 0)),
        _const_spec(wout.shape), _const_spec(n2g.shape), _const_spec(wgu.shape), _const_spec(wd.shape),
    ]
    args = list(mix_inputs) + [x, mod, wout, n2g, wgu, wd]
    if final_norm:
        in_specs.append(_const_spec(final_g.shape))
        args.append(final_g)
    return pl.pallas_call(
        functools.partial(_post_kernel, n_mix=len(mix_inputs), final_norm=final_norm),
        grid=(b, s // t),
        in_specs=in_specs,
        out_specs=tok(d),
        out_shape=jax.ShapeDtypeStruct((b, s, d), F32),
        scratch_shapes=[pltpu.VMEM((t, d), F32)],
        compiler_params=pltpu.CompilerParams(
            dimension_semantics=("arbitrary", "arbitrary"), vmem_limit_bytes=VMEM_LIMIT),
        name="post_final" if final_norm else "post",
    )(*args)


def _mla_slab(nope, rope):
    split = MLA_X1_LANE
    half = MLA_ROPE // 2
    pad = jnp.zeros(nope.shape[:-1] + (LANES - MLA_NOPE - MLA_ROPE,), nope.dtype)
    return jnp.concatenate([nope[..., :split], rope[..., :half], nope[..., split:], pad, rope[..., half:]], axis=-1)


def _diff_qkv_weights(w_qkv):
    half = ROT_DIFF // 2
    c0, c1 = 0, DIFF_HD
    order = (list(range(c0, c0 + half)) + list(range(c1, c1 + half)) + list(range(c0 + ROT_DIFF, c0 + DIFF_HD))
             + list(range(c0 + half, c0 + ROT_DIFF)) + list(range(c1 + half, c1 + ROT_DIFF))
             + list(range(c1 + ROT_DIFF, c1 + DIFF_HD)))
    assert sorted(order) == list(range(LANES))
    width = DIFF_HEADS * LANES
    perm = jnp.asarray([hd * LANES + o for hd in range(DIFF_HEADS) for o in order], jnp.int32)
    return jnp.concatenate([w_qkv[:, :width][:, perm], w_qkv[:, width:2 * width][:, perm], w_qkv[:, 2 * width:]],
                           axis=1).astype(BF16)


def _ffn_weights(w_gate_up, w_down):
    return w_gate_up.astype(BF16), w_down.astype(BF16)


def kernel(x, c, positions, ada_w, ada_b, norm1_g, norm2_g, ffn_w_gate_up, ffn_w_down, mla_w_in, mla_q_norm_g,
           mla_kv_norm_g, mla_w_uq, mla_w_ukv, pool_w, pool_b, pool_scale, mix_a_w_out, diff_w_qkv,
           diff_lambda_q1, diff_lambda_k1, diff_lambda_q2, diff_lambda_k2, diff_subln_g, diff_w_out,
           final_norm_g):
    d = D_MODEL
    rope_tab = _rope_table(positions)
    mod = _modulation(c, ada_w, ada_b)

    w_in = mla_w_in[0]
    o_kv, o_kr, o_u = Q_LORA, Q_LORA + KV_LORA, Q_LORA + KV_LORA + MLA_ROPE
    w_kr = _mla_slab(jnp.zeros((d, MLA_NOPE), F32), w_in[:, o_kr:o_u])
    w_in_ext = jnp.concatenate([w_in[:, :o_kr], w_kr, w_in[:, o_u:]], axis=1).astype(BF16)
    w_uq = mla_w_uq[0].reshape(Q_LORA, MLA_HEADS, MLA_NOPE + MLA_ROPE)
    wq = _mla_slab(w_uq[..., :MLA_NOPE], w_uq[..., MLA_NOPE:]).reshape(Q_LORA, MLA_HEADS * LANES).astype(BF16)
    w_ukv = mla_w_ukv[0].reshape(KV_LORA, MLA_HEADS, MLA_NOPE + MLA_V)
    wk = _mla_slab(w_ukv[..., :MLA_NOPE], jnp.zeros((KV_LORA, MLA_HEADS, MLA_ROPE), F32))
    wk = wk.reshape(KV_LORA, MLA_HEADS * LANES)
    wv = w_ukv[..., MLA_NOPE:].reshape(KV_LORA, MLA_HEADS * MLA_V)
    wkv = jnp.concatenate([wk, wv], axis=1).astype(BF16)
    q0, k0, v0, pool = _l0_pre(
        x, mod[0], norm1_g[0:1], w_in_ext, mla_q_norm_g, mla_kv_norm_g, wq, wkv, rope_tab,
        pool_w[0].astype(BF16), pool_b[0].reshape(1, POOL_WIDTH), pool_scale)
    attn0 = _mla_attention(q0, k0, v0)
    wgu0, wd0 = _ffn_weights(ffn_w_gate_up[0], ffn_w_down[0])
    x = _post([attn0, pool], x, mod[0], mix_a_w_out[0].astype(BF16), norm2_g[0:1], wgu0, wd0)

    lambda_init = 0.8 - 0.6 * math.exp(-0.3 * 1)
    q1, k1, v1 = _l1_pre(x, mod[1], norm1_g[1:2], _diff_qkv_weights(diff_w_qkv[0]), rope_tab)
    lam_vecs = jnp.concatenate([diff_lambda_q1, diff_lambda_k1, diff_lambda_q2, diff_lambda_k2], axis=0)
    attn1 = _diff_attention(q1, k1, v1, lam_vecs, diff_subln_g, lambda_init)
    wgu1, wd1 = _ffn_weights(ffn_w_gate_up[1], ffn_w_down[1])
    return _post([attn1], x, mod[1], diff_w_out[0].astype(BF16), norm2_g[1:2], wgu1, wd1,
                 final_g=final_norm_g.reshape(1, d))
```

```python
import functools
import math

import jax
import jax.numpy as jnp
from jax import lax
from jax.experimental import pallas as pl
from jax.experimental.pallas import tpu as pltpu

D_MODEL = 1024
EPS = 1e-6
ROPE_THETA = 500000.0
MLA_HEADS = 8
MLA_NOPE = 64
MLA_ROPE = 32
MLA_V = 64
Q_LORA = 384
KV_LORA = 256
POOL_WINDOWS = (2, 4, 8, 16)
POOL_GROUP = 128
POOL_WIDTH = POOL_GROUP * len(POOL_WINDOWS)
DIFF_HEADS = 8
DIFF_HD = 64
ROT_DIFF = DIFF_HD // 4
D_FF = 2816

LANES = 128
POOL_HALO = 16
FF_CHUNK = 256
N_FF_CHUNKS = D_FF // FF_CHUNK
assert N_FF_CHUNKS * FF_CHUNK == D_FF

TOK_TILE = 1024
POST_TILE = 1024
MLA_TQ = 512
DIFF_TQ = 256
ATTN_TK = 512
ATTN_GROUP = 8
VMEM_LIMIT = 56 * 1024 * 1024

LOG2E = 1.4426950408889634
MASK_VALUE = -1e30

F32 = jnp.float32
BF16 = jnp.bfloat16


def _const_spec(shape):
    nd = len(shape)
    return pl.BlockSpec(shape, lambda *_: (0,) * nd, pipeline_mode=pl.Buffered(1))


def _rms(x, g):
    return x * lax.rsqrt(jnp.mean(x * x, axis=-1, keepdims=True) + EPS) * g


ROPE_PAIR = LANES // 2
MLA_X1_LANE = ROPE_PAIR - MLA_ROPE // 2


def _rope_slab(x, cos_t, sin_t):
    return x * cos_t + pltpu.roll(x, ROPE_PAIR, 1) * sin_t


TAB_FREQS = 32
TAB_COS_MLA, TAB_COS_DIFF = 0, 16
TAB_SIN_MLA, TAB_SIN_DIFF = TAB_FREQS, TAB_FREQS + 16
TAB_SLOTS = LANES // TAB_FREQS


def _rope_table_kernel(pos_ref, inv_ref, tab_ref):
    ang = pos_ref[...].astype(F32) * inv_ref[...]
    cos = jnp.cos(ang)
    sin = jnp.sin(ang)
    lane = lax.broadcasted_iota(jnp.int32, ang.shape, 1)
    for slot in range(TAB_SLOTS):
        shift_c = (LANES - slot * TAB_FREQS) % LANES
        shift_s = (LANES + TAB_FREQS - slot * TAB_FREQS) % LANES
        c = cos if shift_c == 0 else pltpu.roll(cos, shift_c, 1)
        s = sin if shift_s == 0 else pltpu.roll(sin, shift_s, 1)
        tab_ref[slot] = jnp.where(lane < TAB_FREQS, c, jnp.where(lane < 2 * TAB_FREQS, s, 0.0))


def _rope_table(positions):
    b, s = positions.shape
    rows = b * s // TAB_SLOTS
    inv_mla = ROPE_THETA ** (-jnp.arange(0, MLA_ROPE, 2, dtype=F32) / MLA_ROPE)
    inv_diff = ROPE_THETA ** (-jnp.arange(0, ROT_DIFF, 2, dtype=F32) / ROT_DIFF)
    inv = jnp.concatenate([inv_mla, inv_diff, inv_diff])
    inv = jnp.tile(inv, TAB_SLOTS).reshape(1, LANES)
    pos = jnp.broadcast_to(positions.reshape(TAB_SLOTS, rows).T[:, :, None], (rows, TAB_SLOTS, TAB_FREQS))
    pos = pos.reshape(rows, LANES)
    blk = 1024
    tab = pl.pallas_call(
        _rope_table_kernel,
        grid=(rows // blk,),
        in_specs=[pl.BlockSpec((blk, LANES), lambda i: (i, 0)), _const_spec((1, LANES))],
        out_specs=pl.BlockSpec((TAB_SLOTS, blk, LANES), lambda i: (0, i, 0)),
        out_shape=jax.ShapeDtypeStruct((TAB_SLOTS, rows, LANES), F32),
        name="rope_table",
    )(pos, inv)
    return tab.reshape(b, s, LANES)


def _lane_pattern(tab, pieces, default):
    lane = lax.broadcasted_iota(jnp.int32, (1, LANES), 1)
    out = jnp.full_like(tab, default)
    for lo, hi, src, sign in pieces:
        rolled = pltpu.roll(tab, (lo - src) % LANES, 1)
        out = jnp.where((lane >= lo) & (lane < hi), -rolled if sign < 0 else rolled, out)
    return out


def _rope_patterns(tab, cos_src, sin_src, first, width):
    lo2 = first + ROPE_PAIR
    cos_t = _lane_pattern(tab, [(first, first + width, cos_src, 1), (lo2, lo2 + width, cos_src, 1)], 1.0)
    sin_t = _lane_pattern(tab, [(first, first + width, sin_src, -1), (lo2, lo2 + width, sin_src, 1)], 0.0)
    return cos_t, sin_t


def _mod_kernel(c_ref, w_ref, b_ref, o_ref):
    c = c_ref[...]
    cond = (c * jax.nn.sigmoid(c)).astype(BF16)
    o_ref[0] = jnp.dot(cond, w_ref[0].astype(BF16), preferred_element_type=F32) + b_ref[0]


def _modulation(c, ada_w, ada_b):
    depth, d, n = ada_w.shape
    b = c.shape[0]
    blk = 1536
    mod = pl.pallas_call(
        _mod_kernel,
        grid=(depth, n // blk),
        in_specs=[
            _const_spec((b, d)),
            pl.BlockSpec((1, d, blk), lambda i, j: (i, 0, j)),
            pl.BlockSpec((1, 1, blk), lambda i, j: (i, 0, j)),
        ],
        out_specs=pl.BlockSpec((1, b, blk), lambda i, j: (i, 0, j)),
        out_shape=jax.ShapeDtypeStruct((depth, b, n), F32),
        compiler_params=pltpu.CompilerParams(vmem_limit_bytes=VMEM_LIMIT),
        name="adaln_mod",
    )(c, ada_w, ada_b.reshape(depth, 1, n))
    return mod.reshape(depth, b, 6, d)


def _l0_pre_kernel(x_ref, mod_ref, n1g_ref, win_ref, qg_ref, kvg_ref, wq_ref, wkv_ref, tab_ref,
                   pw_ref, pb_ref, ps_ref, q_out, k_out, v_out, pool_out, halo_sc, *, q_scale):
    si = pl.program_id(1)
    t = x_ref.shape[1]

    @pl.when(si == 0)
    def _():
        halo_sc[...] = jnp.zeros_like(halo_sc)

    mod = mod_ref[0]
    h = _rms(x_ref[0], n1g_ref[...]) * (1.0 + mod[1:2]) + mod[0:1]
    h = h.astype(BF16)
    n_lat = Q_LORA + KV_LORA + LANES
    proj = jnp.dot(h, win_ref[:, :n_lat], preferred_element_type=F32)
    u = jnp.dot(h, win_ref[:, n_lat:], preferred_element_type=F32)
    c_q = proj[:, :Q_LORA]
    c_kv = proj[:, Q_LORA:Q_LORA + KV_LORA]
    k_rope = proj[:, Q_LORA + KV_LORA:]

    cos_t, sin_t = _rope_patterns(tab_ref[0], TAB_COS_MLA, TAB_SIN_MLA, MLA_X1_LANE, MLA_ROPE // 2)

    q = jnp.dot(_rms(c_q, qg_ref[...]).astype(BF16), wq_ref[...], preferred_element_type=F32)
    for hd in range(MLA_HEADS):
        sl = slice(hd * LANES, (hd + 1) * LANES)
        q_out[0, :, sl] = (_rope_slab(q[:, sl], cos_t, sin_t) * q_scale).astype(BF16)

    kv = jnp.dot(_rms(c_kv, kvg_ref[...]).astype(BF16), wkv_ref[...], preferred_element_type=F32)
    k_rope = _rope_slab(k_rope, cos_t, sin_t)
    for hd in range(MLA_HEADS):
        sl = slice(hd * LANES, (hd + 1) * LANES)
        k_out[0, :, sl] = (kv[:, sl] + k_rope).astype(BF16)
    v_out[0] = kv[:, MLA_HEADS * LANES:].astype(BF16)

    ext = jnp.concatenate([halo_sc[...], u], axis=0)
    halo_sc[...] = u[t - POOL_HALO:, :]
    tok = si * t + lax.broadcasted_iota(jnp.int32, (t, 1), 0)
    for g, win in enumerate(POOL_WINDOWS):
        sl = slice(g * POOL_GROUP, (g + 1) * POOL_GROUP)
        e = ext[:, sl]
        shift = 1
        while shift < win:
            e = e + pltpu.roll(e, shift, 0)
            shift *= 2
        cnt = jnp.minimum(tok + 1, win).astype(F32)
        pooled = e[POOL_HALO:] * (1.0 / cnt) - u[:, sl]
        y = jnp.dot(pooled.astype(BF16), pw_ref[g], preferred_element_type=F32) + pb_ref[:, sl]
        pool_out[0, :, sl] = (y * ps_ref[:, sl]).astype(BF16)


def _l0_pre(x, mod, n1g, w_in_ext, qg, kvg, wq, wkv, rope_tab, pw, pb, ps):
    b, s, d = x.shape
    t = TOK_TILE
    tok = lambda w: pl.BlockSpec((1, t, w), lambda i, j: (i, j, 0))
    q_scale = (MLA_NOPE + MLA_ROPE) ** -0.5 * LOG2E
    return pl.pallas_call(
        functools.partial(_l0_pre_kernel, q_scale=q_scale),
        grid=(b, s // t),
        in_specs=[
            tok(d),
            pl.BlockSpec((1, 6, d), lambda i, j: (i, 0, 0)),
            _const_spec(n1g.shape), _const_spec(w_in_ext.shape), _const_spec(qg.shape), _const_spec(kvg.shape),
            _const_spec(wq.shape), _const_spec(wkv.shape),
            tok(LANES),
            _const_spec(pw.shape), _const_spec(pb.shape), _const_spec(ps.shape),
        ],
        out_specs=[tok(MLA_HEADS * LANES), tok(MLA_HEADS * LANES), tok(MLA_HEADS * MLA_V), tok(POOL_WIDTH)],
        out_shape=[
            jax.ShapeDtypeStruct((b, s, MLA_HEADS * LANES), BF16),
            jax.ShapeDtypeStruct((b, s, MLA_HEADS * LANES), BF16),
            jax.ShapeDtypeStruct((b, s, MLA_HEADS * MLA_V), BF16),
            jax.ShapeDtypeStruct((b, s, POOL_WIDTH), BF16),
        ],
        scratch_shapes=[pltpu.VMEM((POOL_HALO, POOL_WIDTH), F32)],
        compiler_params=pltpu.CompilerParams(
            dimension_semantics=("arbitrary", "arbitrary"), vmem_limit_bytes=VMEM_LIMIT),
        name="l0_pre",
    )(x, mod, n1g, w_in_ext, qg, kvg, wq, wkv, rope_tab, pw, pb, ps)


def _l1_pre_kernel(x_ref, mod_ref, n1g_ref, wqkv_ref, tab_ref, q_out, k_out, v_out, *, q_scale):
    t = x_ref.shape[1]
    mod = mod_ref[0]
    h = (_rms(x_ref[0], n1g_ref[...]) * (1.0 + mod[1:2]) + mod[0:1]).astype(BF16)
    cos_t, sin_t = _rope_patterns(tab_ref[0], TAB_COS_DIFF, TAB_SIN_DIFF, 0, ROT_DIFF)
    width = DIFF_HEADS * LANES
    q = jnp.dot(h, wqkv_ref[:, :width], preferred_element_type=F32)
    for hd in range(DIFF_HEADS):
        sl = slice(hd * LANES, (hd + 1) * LANES)
        q_out[0, :, sl] = (_rope_slab(q[:, sl], cos_t, sin_t) * q_scale).astype(BF16)
    k = jnp.dot(h, wqkv_ref[:, width:2 * width], preferred_element_type=F32)
    for hd in range(DIFF_HEADS):
        sl = slice(hd * LANES, (hd + 1) * LANES)
        k_out[0, :, sl] = _rope_slab(k[:, sl], cos_t, sin_t).astype(BF16)
    v_out[0] = jnp.dot(h, wqkv_ref[:, 2 * width:], preferred_element_type=F32).astype(BF16)


def _l1_pre(x, mod, n1g, wqkv, rope_tab):
    b, s, d = x.shape
    t = TOK_TILE
    tok = lambda w: pl.BlockSpec((1, t, w), lambda i, j: (i, j, 0))
    width = DIFF_HEADS * LANES
    return pl.pallas_call(
        functools.partial(_l1_pre_kernel, q_scale=DIFF_HD ** -0.5 * LOG2E),
        grid=(b, s // t),
        in_specs=[
            tok(d),
            pl.BlockSpec((1, 6, d), lambda i, j: (i, 0, 0)),
            _const_spec(n1g.shape), _const_spec(wqkv.shape),
            tok(LANES),
        ],
        out_specs=[tok(width)] * 3,
        out_shape=[jax.ShapeDtypeStruct((b, s, width), BF16)] * 3,
        compiler_params=pltpu.CompilerParams(
            dimension_semantics=("arbitrary", "arbitrary"), vmem_limit_bytes=VMEM_LIMIT),
        name="l1_pre",
    )(x, mod, n1g, wqkv, rope_tab)


def _flash_group(qs, k_ats, v_ats, n_full, row_pos, m_sc, acc_sc, p_sc, alpha_sc):
    n_grp = len(qs)
    n_blk = ATTN_TK // LANES
    acc_blk = acc_sc.shape[-1] // LANES
    col = n_full * ATTN_TK + lax.broadcasted_iota(jnp.int32, (qs[0].shape[0], ATTN_TK), 1)
    mask_bias = jnp.where(col <= row_pos, 0.0, MASK_VALUE)

    def scores(g, j, first):
        s = lax.dot_general(qs[g], k_ats[g](j), (((1,), (1,)), ((), ())), preferred_element_type=F32)
        if first:
            s = s + mask_bias
        blocks = [s[:, c * LANES:(c + 1) * LANES] for c in range(n_blk)]
        m_blk = blocks[0]
        for blk in blocks[1:]:
            m_blk = jnp.maximum(m_blk, blk)
        m_new = jnp.max(m_blk, axis=1, keepdims=True)
        if first:
            m_new = jnp.broadcast_to(m_new, m_sc.shape[1:])
        else:
            m_prev = m_sc[g]
            m_new = jnp.maximum(m_prev, m_new)
            alpha_sc[g] = jnp.exp2(m_prev - m_new)
        p_sc[g] = jnp.concatenate([jnp.exp2(blk - m_new).astype(BF16) for blk in blocks], axis=1)
        m_sc[g] = m_new

    def values(g, j):
        pv = jnp.dot(p_sc[g], v_ats[g](j), preferred_element_type=F32)
        alpha = alpha_sc[g]
        alpha_w = alpha if acc_blk == 1 else jnp.concatenate([alpha] * acc_blk, axis=1)
        acc_sc[g] = alpha_w * acc_sc[g] + pv

    for g in range(n_grp):
        scores(g, n_full, True)

    def trip(j):
        prev = jnp.where(j == 0, n_full, j - 1)
        for g in range(n_grp):
            values(g, prev)
            scores(g, j, False)

    def two_trips(t, carry):
        trip(2 * t)
        trip(2 * t + 1)
        return carry

    def odd_trip(_, carry):
        trip(n_full - 1)
        return carry

    lax.fori_loop(0, lax.shift_right_logical(n_full, 1), two_trips, 0)
    lax.fori_loop(0, n_full & 1, odd_trip, 0)
    last = jnp.maximum(n_full - 1, 0)
    for g in range(n_grp):
        values(g, last)


def _flash_scratch(rows, acc_width):
    return [pltpu.VMEM((ATTN_GROUP, rows, LANES), F32), pltpu.VMEM((ATTN_GROUP, rows, acc_width), F32),
            pltpu.VMEM((ATTN_GROUP, rows, ATTN_TK), BF16), pltpu.VMEM((ATTN_GROUP, rows, LANES), F32)]


def _zero_at_first_step(acc_sc, alpha_sc):
    @pl.when((pl.program_id(0) == 0) & (pl.program_id(1) == 0))
    def _():
        def zero(g, carry):
            acc_sc[g] = jnp.zeros(acc_sc.shape[1:], F32)
            alpha_sc[g] = jnp.zeros(alpha_sc.shape[1:], F32)
            return carry

        lax.fori_loop(0, acc_sc.shape[0], zero, 0)


def _key_step(ref, lanes):
    return lambda j: ref[0, pl.ds(pl.multiple_of(j * ATTN_TK, ATTN_TK), ATTN_TK), lanes]


def _mla_attn_kernel(q_ref, k_ref, v_ref, o_ref, m_sc, acc_sc, p_sc, alpha_sc):
    tq = q_ref.shape[1]
    qi = pl.program_id(1)
    n_full = (qi * tq) // ATTN_TK
    row_pos = qi * tq + lax.broadcasted_iota(jnp.int32, (tq, 1), 0)
    lane = lax.broadcasted_iota(jnp.int32, (tq, LANES), 1)
    one = jnp.ones((ATTN_TK, LANES), BF16)
    n_grp = m_sc.shape[0]
    _zero_at_first_step(acc_sc, alpha_sc)

    def value_step(hd):
        raw = _key_step(v_ref, slice((hd // 2) * LANES, (hd // 2 + 1) * LANES))
        return lambda j: jnp.concatenate([raw(j), one], axis=1)

    for first in range(0, MLA_HEADS, n_grp):
        heads = range(first, first + n_grp)
        slabs = [slice(hd * LANES, (hd + 1) * LANES) for hd in heads]
        _flash_group([q_ref[0, :, sl] for sl in slabs], [_key_step(k_ref, sl) for sl in slabs],
                     [value_step(hd) for hd in heads], n_full, row_pos, m_sc, acc_sc, p_sc, alpha_sc)
        for g in range(0, n_grp, 2):
            even = acc_sc[g, :, :LANES] / acc_sc[g, :, LANES:]
            odd = acc_sc[g + 1, :, :LANES] / acc_sc[g + 1, :, LANES:]
            pair = (first + g) // 2
            o_ref[0, :, pair * LANES:(pair + 1) * LANES] = jnp.where(lane < MLA_V, even, odd).astype(BF16)
            acc_sc[g] = jnp.zeros(acc_sc.shape[1:], F32)
            acc_sc[g + 1] = jnp.zeros(acc_sc.shape[1:], F32)


def _mla_attention(q, k, v):
    b, s, _ = q.shape
    tq = MLA_TQ
    return pl.pallas_call(
        _mla_attn_kernel,
        grid=(b, s // tq),
        in_specs=[
            pl.BlockSpec((1, tq, q.shape[2]), lambda i, j: (i, j, 0)),
            pl.BlockSpec((1, s, k.shape[2]), lambda i, j: (i, 0, 0)),
            pl.BlockSpec((1, s, v.shape[2]), lambda i, j: (i, 0, 0)),
        ],
        out_specs=pl.BlockSpec((1, tq, v.shape[2]), lambda i, j: (i, j, 0)),
        out_shape=jax.ShapeDtypeStruct(v.shape, BF16),
        scratch_shapes=_flash_scratch(tq, 2 * LANES),
        compiler_params=pltpu.CompilerParams(
            dimension_semantics=("arbitrary", "arbitrary"), vmem_limit_bytes=VMEM_LIMIT),
        name="mla_attention",
    )(q, k, v)


def _diff_attn_kernel(q_ref, k_ref, v_ref, lam_ref, g_ref, o_ref, m_sc, acc_sc, p_sc, alpha_sc, *, lambda_init):
    tq = q_ref.shape[1]
    qi = pl.program_id(1)
    n_full = (qi * tq) // ATTN_TK
    row = lax.broadcasted_iota(jnp.int32, (2 * tq, 1), 0)
    row_pos = qi * tq + jnp.where(row >= tq, row - tq, row)
    lane = lax.broadcasted_iota(jnp.int32, (tq, LANES), 1)
    half = ROT_DIFF // 2
    comp0 = (lane < half) | ((lane >= ROT_DIFF) & (lane < DIFF_HD + half))
    one = jnp.ones((ATTN_TK, LANES), BF16)
    _zero_at_first_step(acc_sc, alpha_sc)
    lv = lam_ref[...]
    lam = (jnp.exp(jnp.sum(lv[0:1] * lv[1:2], axis=1, keepdims=True))
           - jnp.exp(jnp.sum(lv[2:3] * lv[3:4], axis=1, keepdims=True)) + lambda_init)
    n_grp = m_sc.shape[0]
    for first in range(0, DIFF_HEADS, n_grp):
        slabs = [slice(hd * LANES, (hd + 1) * LANES) for hd in range(first, first + n_grp)]
        qs = []
        for sl in slabs:
            qh = q_ref[0, :, sl]
            zero = jnp.zeros_like(qh)
            qs.append(jnp.concatenate(
                [jnp.where(comp0, qh, zero), jnp.where(comp0, zero, qh)], axis=0))
        v_ats = [lambda j, raw=_key_step(v_ref, sl): jnp.concatenate([raw(j), one], axis=1) for sl in slabs]
        _flash_group(qs, [_key_step(k_ref, sl) for sl in slabs], v_ats, n_full, row_pos,
                     m_sc, acc_sc, p_sc, alpha_sc)
        for g, sl in enumerate(slabs):
            o = acc_sc[g, :, :LANES] / acc_sc[g, :, LANES:]
            o = o[:tq] - lam * o[tq:]
            o_ref[0, :, sl] = (_rms(o, g_ref[...]) * (1.0 - lambda_init)).astype(BF16)
            acc_sc[g] = jnp.zeros(acc_sc.shape[1:], F32)


def _diff_attention(q, k, v, lam_vecs, subln_g, lambda_init):
    b, s, w = q.shape
    tq = DIFF_TQ
    return pl.pallas_call(
        functools.partial(_diff_attn_kernel, lambda_init=lambda_init),
        grid=(b, s // tq),
        in_specs=[
            pl.BlockSpec((1, tq, w), lambda i, j: (i, j, 0)),
            pl.BlockSpec((1, s, w), lambda i, j: (i, 0, 0)),
            pl.BlockSpec((1, s, w), lambda i, j: (i, 0, 0)),
            _const_spec(lam_vecs.shape), _const_spec(subln_g.shape),
        ],
        out_specs=pl.BlockSpec((1, tq, w), lambda i, j: (i, j, 0)),
        out_shape=jax.ShapeDtypeStruct((b, s, w), BF16),
        scratch_shapes=_flash_scratch(2 * tq, 2 * LANES),
        compiler_params=pltpu.CompilerParams(
            dimension_semantics=("arbitrary", "arbitrary"), vmem_limit_bytes=VMEM_LIMIT),
        name="diff_attention",
    )(q, k, v, lam_vecs, subln_g)


def _post_kernel(*refs, n_mix, final_norm):
    mix_refs = refs[:n_mix]
    x_ref, mod_ref, wout_ref, n2g_ref, wgu_ref, wd_ref = refs[n_mix:n_mix + 6]
    rest = refs[n_mix + 6:]
    if final_norm:
        fng_ref, o_ref, acc_sc = rest
    else:
        o_ref, acc_sc = rest
    mod = mod_ref[0]
    y = None
    row = 0
    for r in mix_refs:
        w = r.shape[2]
        part = jnp.dot(r[0], wout_ref[row:row + w, :], preferred_element_type=F32)
        y = part if y is None else y + part
        row += w
    x1 = x_ref[0] + mod[2:3] * y
    h = (_rms(x1, n2g_ref[...]) * (1.0 + mod[4:5]) + mod[3:4]).astype(BF16)

    for c in range(N_FF_CHUNKS):
        cols = slice(c * FF_CHUNK, (c + 1) * FF_CHUNK)
        ucols = slice(D_FF + c * FF_CHUNK, D_FF + (c + 1) * FF_CHUNK)
        g = jnp.dot(h, wgu_ref[0, :, cols], preferred_element_type=F32)
        u = jnp.dot(h, wgu_ref[0, :, ucols], preferred_element_type=F32)
        act = (g * jax.nn.sigmoid(g) * u).astype(BF16)
        down = jnp.dot(act, wd_ref[0, cols, :], preferred_element_type=F32)
        if c == 0:
            acc_sc[...] = down
        else:
            acc_sc[...] += down
    x2 = x1 + mod[5:6] * acc_sc[...]
    if final_norm:
        x2 = _rms(x2, fng_ref[...])
    o_ref[0] = x2


def _post(mix_inputs, x, mod, wout, n2g, wgu, wd, layer, final_g=None):
    b, s, d = x.shape
    t = POST_TILE
    tok = lambda w: pl.BlockSpec((1, t, w), lambda i, j: (i, j, 0))
    layer_spec = lambda a: pl.BlockSpec((1,) + a.shape[1:], lambda i, j: (layer, 0, 0),
                                        pipeline_mode=pl.Buffered(1))
    final_norm = final_g is not None
    in_specs = [tok(m.shape[2]) for m in mix_inputs] + [
        tok(d),
        pl.BlockSpec((1, 6, d), lambda i, j: (i, 0, 0)),
        _const_spec(wout.shape), _const_spec(n2g.shape), layer_spec(wgu), layer_spec(wd),
    ]
    args = list(mix_inputs) + [x, mod, wout, n2g, wgu, wd]
    if final_norm:
        in_specs.append(_const_spec(final_g.shape))
        args.append(final_g)
    return pl.pallas_call(
        functools.partial(_post_kernel, n_mix=len(mix_inputs), final_norm=final_norm),
        grid=(b, s // t),
        in_specs=in_specs,
        out_specs=tok(d),
        out_shape=jax.ShapeDtypeStruct((b, s, d), F32),
        scratch_shapes=[pltpu.VMEM((t, d), F32)],
        compiler_params=pltpu.CompilerParams(
            dimension_semantics=("arbitrary", "arbitrary"), vmem_limit_bytes=VMEM_LIMIT),
        name="post_final" if final_norm else "post",
    )(*args)


def _mla_slab(nope, rope):
    split = MLA_X1_LANE
    half = MLA_ROPE // 2
    pad = jnp.zeros(nope.shape[:-1] + (LANES - MLA_NOPE - MLA_ROPE,), nope.dtype)
    return jnp.concatenate([nope[..., :split], rope[..., :half], nope[..., split:], pad, rope[..., half:]], axis=-1)


def _diff_qkv_weights(w_qkv):
    half = ROT_DIFF // 2
    c0, c1 = 0, DIFF_HD
    order = (list(range(c0, c0 + half)) + list(range(c1, c1 + half)) + list(range(c0 + ROT_DIFF, c0 + DIFF_HD))
             + list(range(c0 + half, c0 + ROT_DIFF)) + list(range(c1 + half, c1 + ROT_DIFF))
             + list(range(c1 + ROT_DIFF, c1 + DIFF_HD)))
    assert sorted(order) == list(range(LANES))
    width = DIFF_HEADS * LANES
    perm = jnp.asarray([hd * LANES + o for hd in range(DIFF_HEADS) for o in order], jnp.int32)
    return jnp.concatenate([w_qkv[:, :width][:, perm], w_qkv[:, width:2 * width][:, perm], w_qkv[:, 2 * width:]],
                           axis=1).astype(BF16)


def kernel(x, c, positions, ada_w, ada_b, norm1_g, norm2_g, ffn_w_gate_up, ffn_w_down, mla_w_in, mla_q_norm_g,
           mla_kv_norm_g, mla_w_uq, mla_w_ukv, pool_w, pool_b, pool_scale, mix_a_w_out, diff_w_qkv,
           diff_lambda_q1, diff_lambda_k1, diff_lambda_q2, diff_lambda_k2, diff_subln_g, diff_w_out,
           final_norm_g):
    d = D_MODEL
    rope_tab = _rope_table(positions)
    mod = _modulation(c, ada_w, ada_b)

    w_in = mla_w_in[0]
    o_kv, o_kr, o_u = Q_LORA, Q_LORA + KV_LORA, Q_LORA + KV_LORA + MLA_ROPE
    w_kr = _mla_slab(jnp.zeros((d, MLA_NOPE), F32), w_in[:, o_kr:o_u])
    w_in_ext = jnp.concatenate([w_in[:, :o_kr], w_kr, w_in[:, o_u:]], axis=1).astype(BF16)
    w_uq = mla_w_uq[0].reshape(Q_LORA, MLA_HEADS, MLA_NOPE + MLA_ROPE)
    wq = _mla_slab(w_uq[..., :MLA_NOPE], w_uq[..., MLA_NOPE:]).reshape(Q_LORA, MLA_HEADS * LANES).astype(BF16)
    w_ukv = mla_w_ukv[0].reshape(KV_LORA, MLA_HEADS, MLA_NOPE + MLA_V)
    wk = _mla_slab(w_ukv[..., :MLA_NOPE], jnp.zeros((KV_LORA, MLA_HEADS, MLA_ROPE), F32))
    wk = wk.reshape(KV_LORA, MLA_HEADS * LANES)
    wv = w_ukv[..., MLA_NOPE:].reshape(KV_LORA, MLA_HEADS * MLA_V)
    wkv = jnp.concatenate([wk, wv], axis=1).astype(BF16)
    q0, k0, v0, pool = _l0_pre(
        x, mod[0], norm1_g[0:1], w_in_ext, mla_q_norm_g, mla_kv_norm_g, wq, wkv, rope_tab,
        pool_w[0].astype(BF16), pool_b[0].reshape(1, POOL_WIDTH), pool_scale)
    attn0 = _mla_attention(q0, k0, v0)
    wgu, wd = ffn_w_gate_up.astype(BF16), ffn_w_down.astype(BF16)
    x = _post([attn0, pool], x, mod[0], mix_a_w_out[0].astype(BF16), norm2_g[0:1], wgu, wd, 0)

    lambda_init = 0.8 - 0.6 * math.exp(-0.3 * 1)
    q1, k1, v1 = _l1_pre(x, mod[1], norm1_g[1:2], _diff_qkv_weights(diff_w_qkv[0]), rope_tab)
    lam_vecs = jnp.concatenate([diff_lambda_q1, diff_lambda_k1, diff_lambda_q2, diff_lambda_k2], axis=0)
    attn1 = _diff_attention(q1, k1, v1, lam_vecs, diff_subln_g, lambda_init)
    return _post([attn1], x, mod[1], diff_w_out[0].astype(BF16), norm2_g[1:2], wgu, wd, 1,
                 final_g=final_norm_g.reshape(1, d))
```

```python
import functools
import math

import jax
import jax.numpy as jnp
from jax import lax
from jax.experimental import pallas as pl
from jax.experimental.pallas import tpu as pltpu

D_MODEL = 1024
EPS = 1e-6
ROPE_THETA = 500000.0
MLA_HEADS = 8
MLA_NOPE = 64
MLA_ROPE = 32
MLA_V = 64
Q_LORA = 384
KV_LORA = 256
POOL_WINDOWS = (2, 4, 8, 16)
POOL_GROUP = 128
POOL_WIDTH = POOL_GROUP * len(POOL_WINDOWS)
DIFF_HEADS = 8
DIFF_HD = 64
ROT_DIFF = DIFF_HD // 4
D_FF = 2816

LANES = 128
POOL_HALO = 16
FF_CHUNK = 256
N_FF_CHUNKS = D_FF // FF_CHUNK
assert N_FF_CHUNKS * FF_CHUNK == D_FF

TOK_TILE = 1024
POST_TILE = 1024
MLA_TQ = 512
DIFF_TQ = 256
ATTN_TK = 512
ATTN_GROUP = 8
VMEM_LIMIT = 56 * 1024 * 1024

LOG2E = 1.4426950408889634
MASK_VALUE = -1e30

F32 = jnp.float32
BF16 = jnp.bfloat16


def _const_spec(shape):
    nd = len(shape)
    return pl.BlockSpec(shape, lambda *_: (0,) * nd, pipeline_mode=pl.Buffered(1))


def _rms(x, g):
    return x * lax.rsqrt(jnp.mean(x * x, axis=-1, keepdims=True) + EPS) * g


ROPE_PAIR = LANES // 2
MLA_X1_LANE = ROPE_PAIR - MLA_ROPE // 2


def _rope_slab(x, cos_t, sin_t):
    return x * cos_t + pltpu.roll(x, ROPE_PAIR, 1) * sin_t


TAB_FREQS = 32
TAB_COS_MLA, TAB_COS_DIFF = 0, 16
TAB_SIN_MLA, TAB_SIN_DIFF = TAB_FREQS, TAB_FREQS + 16
TAB_SLOTS = LANES // TAB_FREQS


def _rope_table_kernel(pos_ref, inv_ref, tab_ref):
    ang = pos_ref[...].astype(F32) * inv_ref[...]
    cos = jnp.cos(ang)
    sin = jnp.sin(ang)
    lane = lax.broadcasted_iota(jnp.int32, ang.shape, 1)
    for slot in range(TAB_SLOTS):
        shift_c = (LANES - slot * TAB_FREQS) % LANES
        shift_s = (LANES + TAB_FREQS - slot * TAB_FREQS) % LANES
        c = cos if shift_c == 0 else pltpu.roll(cos, shift_c, 1)
        s = sin if shift_s == 0 else pltpu.roll(sin, shift_s, 1)
        tab_ref[slot] = jnp.where(lane < TAB_FREQS, c, jnp.where(lane < 2 * TAB_FREQS, s, 0.0))


def _rope_table(positions):
    b, s = positions.shape
    rows = b * s // TAB_SLOTS
    inv_mla = ROPE_THETA ** (-jnp.arange(0, MLA_ROPE, 2, dtype=F32) / MLA_ROPE)
    inv_diff = ROPE_THETA ** (-jnp.arange(0, ROT_DIFF, 2, dtype=F32) / ROT_DIFF)
    inv = jnp.concatenate([inv_mla, inv_diff, inv_diff])
    inv = jnp.tile(inv, TAB_SLOTS).reshape(1, LANES)
    pos = jnp.broadcast_to(positions.reshape(TAB_SLOTS, rows).T[:, :, None], (rows, TAB_SLOTS, TAB_FREQS))
    pos = pos.reshape(rows, LANES)
    blk = 1024
    tab = pl.pallas_call(
        _rope_table_kernel,
        grid=(rows // blk,),
        in_specs=[pl.BlockSpec((blk, LANES), lambda i: (i, 0)), _const_spec((1, LANES))],
        out_specs=pl.BlockSpec((TAB_SLOTS, blk, LANES), lambda i: (0, i, 0)),
        out_shape=jax.ShapeDtypeStruct((TAB_SLOTS, rows, LANES), F32),
        name="rope_table",
    )(pos, inv)
    return tab.reshape(b, s, LANES)


def _lane_pattern(tab, pieces, default):
    lane = lax.broadcasted_iota(jnp.int32, (1, LANES), 1)
    out = jnp.full_like(tab, default)
    for lo, hi, src, sign in pieces:
        rolled = pltpu.roll(tab, (lo - src) % LANES, 1)
        out = jnp.where((lane >= lo) & (lane < hi), -rolled if sign < 0 else rolled, out)
    return out


def _rope_patterns(tab, cos_src, sin_src, first, width):
    lo2 = first + ROPE_PAIR
    cos_t = _lane_pattern(tab, [(first, first + width, cos_src, 1), (lo2, lo2 + width, cos_src, 1)], 1.0)
    sin_t = _lane_pattern(tab, [(first, first + width, sin_src, -1), (lo2, lo2 + width, sin_src, 1)], 0.0)
    return cos_t, sin_t


def _mod_kernel(c_ref, w_ref, b_ref, o_ref):
    c = c_ref[...]
    cond = (c * jax.nn.sigmoid(c)).astype(BF16)
    o_ref[0] = jnp.dot(cond, w_ref[0].astype(BF16), preferred_element_type=F32) + b_ref[0]


def _modulation(c, ada_w, ada_b):
    depth, d, n = ada_w.shape
    b = c.shape[0]
    blk = 1536
    mod = pl.pallas_call(
        _mod_kernel,
        grid=(depth, n // blk),
        in_specs=[
            _const_spec((b, d)),
            pl.BlockSpec((1, d, blk), lambda i, j: (i, 0, j)),
            pl.BlockSpec((1, 1, blk), lambda i, j: (i, 0, j)),
        ],
        out_specs=pl.BlockSpec((1, b, blk), lambda i, j: (i, 0, j)),
        out_shape=jax.ShapeDtypeStruct((depth, b, n), F32),
        compiler_params=pltpu.CompilerParams(vmem_limit_bytes=VMEM_LIMIT),
        name="adaln_mod",
    )(c, ada_w, ada_b.reshape(depth, 1, n))
    return mod.reshape(depth, b, 6, d)


def _l0_pre_kernel(x_ref, mod_ref, n1g_ref, win_ref, qg_ref, kvg_ref, wq_ref, wkv_ref, tab_ref,
                   pw_ref, pb_ref, ps_ref, q_out, k_out, v_out, pool_out, halo_sc, *, q_scale):
    si = pl.program_id(1)
    t = x_ref.shape[1]

    @pl.when(si == 0)
    def _():
        halo_sc[...] = jnp.zeros_like(halo_sc)

    mod = mod_ref[0]
    h = _rms(x_ref[0], n1g_ref[...]) * (1.0 + mod[1:2]) + mod[0:1]
    h = h.astype(BF16)
    n_lat = Q_LORA + KV_LORA + LANES
    proj = jnp.dot(h, win_ref[:, :n_lat], preferred_element_type=F32)
    u = jnp.dot(h, win_ref[:, n_lat:], preferred_element_type=F32)
    c_q = proj[:, :Q_LORA]
    c_kv = proj[:, Q_LORA:Q_LORA + KV_LORA]
    k_rope = proj[:, Q_LORA + KV_LORA:]

    cos_t, sin_t = _rope_patterns(tab_ref[0], TAB_COS_MLA, TAB_SIN_MLA, MLA_X1_LANE, MLA_ROPE // 2)

    q = jnp.dot(_rms(c_q, qg_ref[...]).astype(BF16), wq_ref[...], preferred_element_type=F32)
    for hd in range(MLA_HEADS):
        sl = slice(hd * LANES, (hd + 1) * LANES)
        q_out[0, :, sl] = (_rope_slab(q[:, sl], cos_t, sin_t) * q_scale).astype(BF16)

    kv = jnp.dot(_rms(c_kv, kvg_ref[...]).astype(BF16), wkv_ref[...], preferred_element_type=F32)
    k_rope = _rope_slab(k_rope, cos_t, sin_t)
    for hd in range(MLA_HEADS):
        sl = slice(hd * LANES, (hd + 1) * LANES)
        k_out[0, :, sl] = (kv[:, sl] + k_rope).astype(BF16)
    v_out[0] = kv[:, MLA_HEADS * LANES:].astype(BF16)

    ext = jnp.concatenate([halo_sc[...], u], axis=0)
    halo_sc[...] = u[t - POOL_HALO:, :]
    tok = si * t + lax.broadcasted_iota(jnp.int32, (t, 1), 0)
    for g, win in enumerate(POOL_WINDOWS):
        sl = slice(g * POOL_GROUP, (g + 1) * POOL_GROUP)
        e = ext[:, sl]
        shift = 1
        while shift < win:
            e = e + pltpu.roll(e, shift, 0)
            shift *= 2
        cnt = jnp.minimum(tok + 1, win).astype(F32)
        pooled = e[POOL_HALO:] * (1.0 / cnt) - u[:, sl]
        y = jnp.dot(pooled.astype(BF16), pw_ref[g], preferred_element_type=F32) + pb_ref[:, sl]
        pool_out[0, :, sl] = (y * ps_ref[:, sl]).astype(BF16)


def _l0_pre(x, mod, n1g, w_in_ext, qg, kvg, wq, wkv, rope_tab, pw, pb, ps):
    b, s, d = x.shape
    t = TOK_TILE
    tok = lambda w: pl.BlockSpec((1, t, w), lambda i, j: (i, j, 0))
    q_scale = (MLA_NOPE + MLA_ROPE) ** -0.5 * LOG2E
    return pl.pallas_call(
        functools.partial(_l0_pre_kernel, q_scale=q_scale),
        grid=(b, s // t),
        in_specs=[
            tok(d),
            pl.BlockSpec((1, 6, d), lambda i, j: (i, 0, 0)),
            _const_spec(n1g.shape), _const_spec(w_in_ext.shape), _const_spec(qg.shape), _const_spec(kvg.shape),
            _const_spec(wq.shape), _const_spec(wkv.shape),
            tok(LANES),
            _const_spec(pw.shape), _const_spec(pb.shape), _const_spec(ps.shape),
        ],
        out_specs=[tok(MLA_HEADS * LANES), tok(MLA_HEADS * LANES), tok(MLA_HEADS * MLA_V), tok(POOL_WIDTH)],
        out_shape=[
            jax.ShapeDtypeStruct((b, s, MLA_HEADS * LANES), BF16),
            jax.ShapeDtypeStruct((b, s, MLA_HEADS * LANES), BF16),
            jax.ShapeDtypeStruct((b, s, MLA_HEADS * MLA_V), BF16),
            jax.ShapeDtypeStruct((b, s, POOL_WIDTH), BF16),
        ],
        scratch_shapes=[pltpu.VMEM((POOL_HALO, POOL_WIDTH), F32)],
        compiler_params=pltpu.CompilerParams(
            dimension_semantics=("arbitrary", "arbitrary"), vmem_limit_bytes=VMEM_LIMIT),
        name="l0_pre",
    )(x, mod, n1g, w_in_ext, qg, kvg, wq, wkv, rope_tab, pw, pb, ps)


def _l1_pre_kernel(x_ref, mod_ref, n1g_ref, wqkv_ref, tab_ref, q_out, k_out, v_out, *, q_scale):
    t = x_ref.shape[1]
    mod = mod_ref[0]
    h = (_rms(x_ref[0], n1g_ref[...]) * (1.0 + mod[1:2]) + mod[0:1]).astype(BF16)
    cos_t, sin_t = _rope_patterns(tab_ref[0], TAB_COS_DIFF, TAB_SIN_DIFF, 0, ROT_DIFF)
    width = DIFF_HEADS * LANES
    q = jnp.dot(h, wqkv_ref[:, :width], preferred_element_type=F32)
    for hd in range(DIFF_HEADS):
        sl = slice(hd * LANES, (hd + 1) * LANES)
        q_out[0, :, sl] = (_rope_slab(q[:, sl], cos_t, sin_t) * q_scale).astype(BF16)
    k = jnp.dot(h, wqkv_ref[:, width:2 * width], preferred_element_type=F32)
    for hd in range(DIFF_HEADS):
        sl = slice(hd * LANES, (hd + 1) * LANES)
        k_out[0, :, sl] = _rope_slab(k[:, sl], cos_t, sin_t).astype(BF16)
    v_out[0] = jnp.dot(h, wqkv_ref[:, 2 * width:], preferred_element_type=F32).astype(BF16)


def _l1_pre(x, mod, n1g, wqkv, rope_tab):
    b, s, d = x.shape
    t = TOK_TILE
    tok = lambda w: pl.BlockSpec((1, t, w), lambda i, j: (i, j, 0))
    width = DIFF_HEADS * LANES
    return pl.pallas_call(
        functools.partial(_l1_pre_kernel, q_scale=DIFF_HD ** -0.5 * LOG2E),
        grid=(b, s // t),
        in_specs=[
            tok(d),
            pl.BlockSpec((1, 6, d), lambda i, j: (i, 0, 0)),
            _const_spec(n1g.shape), _const_spec(wqkv.shape),
            tok(LANES),
        ],
        out_specs=[tok(width)] * 3,
        out_shape=[jax.ShapeDtypeStruct((b, s, width), BF16)] * 3,
        compiler_params=pltpu.CompilerParams(
            dimension_semantics=("arbitrary", "arbitrary"), vmem_limit_bytes=VMEM_LIMIT),
        name="l1_pre",
    )(x, mod, n1g, wqkv, rope_tab)


def _flash_group(qs, k_ats, v_ats, n_full, row_pos, m_sc, acc_sc, p_sc, alpha_sc):
    n_grp = len(qs)
    n_blk = ATTN_TK // LANES
    acc_blk = acc_sc.shape[-1] // LANES
    col = n_full * ATTN_TK + lax.broadcasted_iota(jnp.int32, (qs[0].shape[0], ATTN_TK), 1)
    mask_bias = jnp.where(col <= row_pos, 0.0, MASK_VALUE)

    def scores(g, j, first):
        s = lax.dot_general(qs[g], k_ats[g](j), (((1,), (1,)), ((), ())), preferred_element_type=F32)
        if first:
            s = s + mask_bias
        blocks = [s[:, c * LANES:(c + 1) * LANES] for c in range(n_blk)]
        m_blk = blocks[0]
        for blk in blocks[1:]:
            m_blk = jnp.maximum(m_blk, blk)
        m_new = jnp.max(m_blk, axis=1, keepdims=True)
        if first:
            m_new = jnp.broadcast_to(m_new, m_sc.shape[1:])
        else:
            m_prev = m_sc[g]
            m_new = jnp.maximum(m_prev, m_new)
            alpha_sc[g] = jnp.exp2(m_prev - m_new)
        p_sc[g] = jnp.concatenate([jnp.exp2(blk - m_new).astype(BF16) for blk in blocks], axis=1)
        m_sc[g] = m_new

    def values(g, j):
        pv = jnp.dot(p_sc[g], v_ats[g](j), preferred_element_type=F32)
        alpha = alpha_sc[g]
        alpha_w = alpha if acc_blk == 1 else jnp.concatenate([alpha] * acc_blk, axis=1)
        acc_sc[g] = alpha_w * acc_sc[g] + pv

    for g in range(n_grp):
        scores(g, n_full, True)

    def trip(j):
        prev = jnp.where(j == 0, n_full, j - 1)
        for g in range(n_grp):
            values(g, prev)
            scores(g, j, False)

    def four_trips(t, carry):
        for k in range(4):
            trip(4 * t + k)
        return carry

    n_quads = lax.shift_right_logical(n_full, 2)

    def two_trips(_, carry):
        trip(4 * n_quads)
        trip(4 * n_quads + 1)
        return carry

    def odd_trip(_, carry):
        trip(n_full - 1)
        return carry

    lax.fori_loop(0, n_quads, four_trips, 0)
    lax.fori_loop(0, lax.shift_right_logical(n_full, 1) & 1, two_trips, 0)
    lax.fori_loop(0, n_full & 1, odd_trip, 0)
    last = jnp.maximum(n_full - 1, 0)
    for g in range(n_grp):
        values(g, last)


def _flash_scratch(rows, acc_width):
    return [pltpu.VMEM((ATTN_GROUP, rows, LANES), F32), pltpu.VMEM((ATTN_GROUP, rows, acc_width), F32),
            pltpu.VMEM((ATTN_GROUP, rows, ATTN_TK), BF16), pltpu.VMEM((ATTN_GROUP, rows, LANES), F32)]


def _zero_at_first_step(acc_sc, alpha_sc):
    @pl.when((pl.program_id(0) == 0) & (pl.program_id(1) == 0))
    def _():
        def zero(g, carry):
            acc_sc[g] = jnp.zeros(acc_sc.shape[1:], F32)
            alpha_sc[g] = jnp.zeros(alpha_sc.shape[1:], F32)
            return carry

        lax.fori_loop(0, acc_sc.shape[0], zero, 0)


def _key_step(ref, lanes):
    return lambda j: ref[0, pl.ds(pl.multiple_of(j * ATTN_TK, ATTN_TK), ATTN_TK), lanes]


def _mla_attn_kernel(q_ref, k_ref, v_ref, o_ref, m_sc, acc_sc, p_sc, alpha_sc):
    tq = q_ref.shape[1]
    qi = pl.program_id(1)
    n_full = (qi * tq) // ATTN_TK
    row_pos = qi * tq + lax.broadcasted_iota(jnp.int32, (tq, 1), 0)
    lane = lax.broadcasted_iota(jnp.int32, (tq, LANES), 1)
    one = jnp.ones((ATTN_TK, LANES), BF16)
    n_grp = m_sc.shape[0]
    _zero_at_first_step(acc_sc, alpha_sc)

    def value_step(hd):
        raw = _key_step(v_ref, slice((hd // 2) * LANES, (hd // 2 + 1) * LANES))
        return lambda j: jnp.concatenate([raw(j), one], axis=1)

    for first in range(0, MLA_HEADS, n_grp):
        heads = range(first, first + n_grp)
        slabs = [slice(hd * LANES, (hd + 1) * LANES) for hd in heads]
        _flash_group([q_ref[0, :, sl] for sl in slabs], [_key_step(k_ref, sl) for sl in slabs],
                     [value_step(hd) for hd in heads], n_full, row_pos, m_sc, acc_sc, p_sc, alpha_sc)
        for g in range(0, n_grp, 2):
            even = acc_sc[g, :, :LANES] / acc_sc[g, :, LANES:]
            odd = acc_sc[g + 1, :, :LANES] / acc_sc[g + 1, :, LANES:]
            pair = (first + g) // 2
            o_ref[0, :, pair * LANES:(pair + 1) * LANES] = jnp.where(lane < MLA_V, even, odd).astype(BF16)
            acc_sc[g] = jnp.zeros(acc_sc.shape[1:], F32)
            acc_sc[g + 1] = jnp.zeros(acc_sc.shape[1:], F32)


def _mla_attention(q, k, v):
    b, s, _ = q.shape
    tq = MLA_TQ
    return pl.pallas_call(
        _mla_attn_kernel,
        grid=(b, s // tq),
        in_specs=[
            pl.BlockSpec((1, tq, q.shape[2]), lambda i, j: (i, j, 0)),
            pl.BlockSpec((1, s, k.shape[2]), lambda i, j: (i, 0, 0)),
            pl.BlockSpec((1, s, v.shape[2]), lambda i, j: (i, 0, 0)),
        ],
        out_specs=pl.BlockSpec((1, tq, v.shape[2]), lambda i, j: (i, j, 0)),
        out_shape=jax.ShapeDtypeStruct(v.shape, BF16),
        scratch_shapes=_flash_scratch(tq, 2 * LANES),
        compiler_params=pltpu.CompilerParams(
            dimension_semantics=("arbitrary", "arbitrary"), vmem_limit_bytes=VMEM_LIMIT),
        name="mla_attention",
    )(q, k, v)


def _diff_attn_kernel(q_ref, k_ref, v_ref, lam_ref, g_ref, o_ref, m_sc, acc_sc, p_sc, alpha_sc, *, lambda_init):
    tq = q_ref.shape[1]
    qi = pl.program_id(1)
    n_full = (qi * tq) // ATTN_TK
    row = lax.broadcasted_iota(jnp.int32, (2 * tq, 1), 0)
    row_pos = qi * tq + jnp.where(row >= tq, row - tq, row)
    lane = lax.broadcasted_iota(jnp.int32, (tq, LANES), 1)
    half = ROT_DIFF // 2
    comp0 = (lane < half) | ((lane >= ROT_DIFF) & (lane < DIFF_HD + half))
    one = jnp.ones((ATTN_TK, LANES), BF16)
    _zero_at_first_step(acc_sc, alpha_sc)
    lv = lam_ref[...]
    lam = (jnp.exp(jnp.sum(lv[0:1] * lv[1:2], axis=1, keepdims=True))
           - jnp.exp(jnp.sum(lv[2:3] * lv[3:4], axis=1, keepdims=True)) + lambda_init)
    n_grp = m_sc.shape[0]
    for first in range(0, DIFF_HEADS, n_grp):
        slabs = [slice(hd * LANES, (hd + 1) * LANES) for hd in range(first, first + n_grp)]
        qs = []
        for sl in slabs:
            qh = q_ref[0, :, sl]
            zero = jnp.zeros_like(qh)
            qs.append(jnp.concatenate(
                [jnp.where(comp0, qh, zero), jnp.where(comp0, zero, qh)], axis=0))
        v_ats = [lambda j, raw=_key_step(v_ref, sl): jnp.concatenate([raw(j), one], axis=1) for sl in slabs]
        _flash_group(qs, [_key_step(k_ref, sl) for sl in slabs], v_ats, n_full, row_pos,
                     m_sc, acc_sc, p_sc, alpha_sc)
        for g, sl in enumerate(slabs):
            o = acc_sc[g, :, :LANES] / acc_sc[g, :, LANES:]
            o = o[:tq] - lam * o[tq:]
            o_ref[0, :, sl] = (_rms(o, g_ref[...]) * (1.0 - lambda_init)).astype(BF16)
            acc_sc[g] = jnp.zeros(acc_sc.shape[1:], F32)


def _diff_attention(q, k, v, lam_vecs, subln_g, lambda_init):
    b, s, w = q.shape
    tq = DIFF_TQ
    return pl.pallas_call(
        functools.partial(_diff_attn_kernel, lambda_init=lambda_init),
        grid=(b, s // tq),
        in_specs=[
            pl.BlockSpec((1, tq, w), lambda i, j: (i, j, 0)),
            pl.BlockSpec((1, s, w), lambda i, j: (i, 0, 0)),
            pl.BlockSpec((1, s, w), lambda i, j: (i, 0, 0)),
            _const_spec(lam_vecs.shape), _const_spec(subln_g.shape),
        ],
        out_specs=pl.BlockSpec((1, tq, w), lambda i, j: (i, j, 0)),
        out_shape=jax.ShapeDtypeStruct((b, s, w), BF16),
        scratch_shapes=_flash_scratch(2 * tq, 2 * LANES),
        compiler_params=pltpu.CompilerParams(
            dimension_semantics=("arbitrary", "arbitrary"), vmem_limit_bytes=VMEM_LIMIT),
        name="diff_attention",
    )(q, k, v, lam_vecs, subln_g)


def _post_kernel(*refs, n_mix, final_norm):
    mix_refs = refs[:n_mix]
    x_ref, mod_ref, wout_ref, n2g_ref, wgu_ref, wd_ref = refs[n_mix:n_mix + 6]
    rest = refs[n_mix + 6:]
    if final_norm:
        fng_ref, o_ref, acc_sc = rest
    else:
        o_ref, acc_sc = rest
    mod = mod_ref[0]
    y = None
    row = 0
    for r in mix_refs:
        w = r.shape[2]
        part = jnp.dot(r[0], wout_ref[row:row + w, :], preferred_element_type=F32)
        y = part if y is None else y + part
        row += w
    x1 = x_ref[0] + mod[2:3] * y
    h = (_rms(x1, n2g_ref[...]) * (1.0 + mod[4:5]) + mod[3:4]).astype(BF16)

    for c in range(N_FF_CHUNKS):
        cols = slice(c * FF_CHUNK, (c + 1) * FF_CHUNK)
        ucols = slice(D_FF + c * FF_CHUNK, D_FF + (c + 1) * FF_CHUNK)
        g = jnp.dot(h, wgu_ref[0, :, cols], preferred_element_type=F32)
        u = jnp.dot(h, wgu_ref[0, :, ucols], preferred_element_type=F32)
        act = (g * jax.nn.sigmoid(g) * u).astype(BF16)
        down = jnp.dot(act, wd_ref[0, cols, :], preferred_element_type=F32)
        if c == 0:
            acc_sc[...] = down
        else:
            acc_sc[...] += down
    x2 = x1 + mod[5:6] * acc_sc[...]
    if final_norm:
        x2 = _rms(x2, fng_ref[...])
    o_ref[0] = x2


def _post(mix_inputs, x, mod, wout, n2g, wgu, wd, layer, final_g=None):
    b, s, d = x.shape
    t = POST_TILE
    tok = lambda w: pl.BlockSpec((1, t, w), lambda i, j: (i, j, 0))
    layer_spec = lambda a: pl.BlockSpec((1,) + a.shape[1:], lambda i, j: (layer, 0, 0),
                                        pipeline_mode=pl.Buffered(1))
    final_norm = final_g is not None
    in_specs = [tok(m.shape[2]) for m in mix_inputs] + [
        tok(d),
        pl.BlockSpec((1, 6, d), lambda i, j: (i, 0, 0)),
        _const_spec(wout.shape), _const_spec(n2g.shape), layer_spec(wgu), layer_spec(wd),
    ]
    args = list(mix_inputs) + [x, mod, wout, n2g, wgu, wd]
    if final_norm:
        in_specs.append(_const_spec(final_g.shape))
        args.append(final_g)
    return pl.pallas_call(
        functools.partial(_post_kernel, n_mix=len(mix_inputs), final_norm=final_norm),
        grid=(b, s // t),
        in_specs=in_specs,
        out_specs=tok(d),
        out_shape=jax.ShapeDtypeStruct((b, s, d), F32),
        scratch_shapes=[pltpu.VMEM((t, d), F32)],
        compiler_params=pltpu.CompilerParams(
            dimension_semantics=("arbitrary", "arbitrary"), vmem_limit_bytes=VMEM_LIMIT),
        name="post_final" if final_norm else "post",
    )(*args)


def _mla_slab(nope, rope):
    split = MLA_X1_LANE
    half = MLA_ROPE // 2
    pad = jnp.zeros(nope.shape[:-1] + (LANES - MLA_NOPE - MLA_ROPE,), nope.dtype)
    return jnp.concatenate([nope[..., :split], rope[..., :half], nope[..., split:], pad, rope[..., half:]], axis=-1)


def _diff_qkv_weights(w_qkv):
    half = ROT_DIFF // 2
    c0, c1 = 0, DIFF_HD
    pieces = [(c0, c0 + half), (c1, c1 + half), (c0 + ROT_DIFF, c0 + DIFF_HD),
              (c0 + half, c0 + ROT_DIFF), (c1 + half, c1 + ROT_DIFF), (c1 + ROT_DIFF, c1 + DIFF_HD)]
    assert sorted(i for lo, hi in pieces for i in range(lo, hi)) == list(range(LANES))
    width = DIFF_HEADS * LANES

    def slab(w):
        w = w.reshape(w.shape[0], DIFF_HEADS, LANES)
        return jnp.concatenate([w[..., lo:hi] for lo, hi in pieces], axis=-1).reshape(w.shape[0], width)

    return jnp.concatenate([slab(w_qkv[:, :width]), slab(w_qkv[:, width:2 * width]), w_qkv[:, 2 * width:]],
                           axis=1).astype(BF16)


def kernel(x, c, positions, ada_w, ada_b, norm1_g, norm2_g, ffn_w_gate_up, ffn_w_down, mla_w_in, mla_q_norm_g,
           mla_kv_norm_g, mla_w_uq, mla_w_ukv, pool_w, pool_b, pool_scale, mix_a_w_out, diff_w_qkv,
           diff_lambda_q1, diff_lambda_k1, diff_lambda_q2, diff_lambda_k2, diff_subln_g, diff_w_out,
           final_norm_g):
    d = D_MODEL
    rope_tab = _rope_table(positions)
    mod = _modulation(c, ada_w, ada_b)

    w_in = mla_w_in[0]
    o_kv, o_kr, o_u = Q_LORA, Q_LORA + KV_LORA, Q_LORA + KV_LORA + MLA_ROPE
    w_kr = _mla_slab(jnp.zeros((d, MLA_NOPE), F32), w_in[:, o_kr:o_u])
    w_in_ext = jnp.concatenate([w_in[:, :o_kr], w_kr, w_in[:, o_u:]], axis=1).astype(BF16)
    w_uq = mla_w_uq[0].reshape(Q_LORA, MLA_HEADS, MLA_NOPE + MLA_ROPE)
    wq = _mla_slab(w_uq[..., :MLA_NOPE], w_uq[..., MLA_NOPE:]).reshape(Q_LORA, MLA_HEADS * LANES).astype(BF16)
    w_ukv = mla_w_ukv[0].reshape(KV_LORA, MLA_HEADS, MLA_NOPE + MLA_V)
    wk = _mla_slab(w_ukv[..., :MLA_NOPE], jnp.zeros((KV_LORA, MLA_HEADS, MLA_ROPE), F32))
    wk = wk.reshape(KV_LORA, MLA_HEADS * LANES)
    wv = w_ukv[..., MLA_NOPE:].reshape(KV_LORA, MLA_HEADS * MLA_V)
    wkv = jnp.concatenate([wk, wv], axis=1).astype(BF16)
    q0, k0, v0, pool = _l0_pre(
        x, mod[0], norm1_g[0:1], w_in_ext, mla_q_norm_g, mla_kv_norm_g, wq, wkv, rope_tab,
        pool_w[0].astype(BF16), pool_b[0].reshape(1, POOL_WIDTH), pool_scale)
    attn0 = _mla_attention(q0, k0, v0)
    wgu, wd = ffn_w_gate_up.astype(BF16), ffn_w_down.astype(BF16)
    x = _post([attn0, pool], x, mod[0], mix_a_w_out[0].astype(BF16), norm2_g[0:1], wgu, wd, 0)

    lambda_init = 0.8 - 0.6 * math.exp(-0.3 * 1)
    q1, k1, v1 = _l1_pre(x, mod[1], norm1_g[1:2], _diff_qkv_weights(diff_w_qkv[0]), rope_tab)
    lam_vecs = jnp.concatenate([diff_lambda_q1, diff_lambda_k1, diff_lambda_q2, diff_lambda_k2], axis=0)
    attn1 = _diff_attention(q1, k1, v1, lam_vecs, diff_subln_g, lambda_init)
    return _post([attn1], x, mod[1], diff_w_out[0].astype(BF16), norm2_g[1:2], wgu, wd, 1,
                 final_g=final_norm_g.reshape(1, d))
```

```python
import functools
import math

import jax
import jax.numpy as jnp
from jax import lax
from jax.experimental import pallas as pl
from jax.experimental.pallas import tpu as pltpu

D_MODEL = 1024
EPS = 1e-6
ROPE_THETA = 500000.0
MLA_HEADS = 8
MLA_NOPE = 64
MLA_ROPE = 32
MLA_V = 64
Q_LORA = 384
KV_LORA = 256
POOL_WINDOWS = (2, 4, 8, 16)
POOL_GROUP = 128
POOL_WIDTH = POOL_GROUP * len(POOL_WINDOWS)
DIFF_HEADS = 8
DIFF_HD = 64
ROT_DIFF = DIFF_HD // 4
D_FF = 2816

LANES = 128
POOL_HALO = 16
FF_CHUNK = 256
N_FF_CHUNKS = D_FF // FF_CHUNK
assert N_FF_CHUNKS * FF_CHUNK == D_FF

TOK_TILE = 1024
POST_TILE = 1024
MLA_TQ = 512
DIFF_TQ = 256
ATTN_TK = 512
ATTN_GROUP = 8
ROPE_ROWS = 1024
MOD_COLS = 1536
VMEM_LIMIT = 56 * 1024 * 1024

LOG2E = 1.4426950408889634
MASK_VALUE = -1e30

F32 = jnp.float32
BF16 = jnp.bfloat16


def _const_spec(shape):
    nd = len(shape)
    return pl.BlockSpec(shape, lambda *_: (0,) * nd, pipeline_mode=pl.Buffered(1))


def _rms(x, g):
    return x * lax.rsqrt(jnp.mean(x * x, axis=-1, keepdims=True) + EPS) * g


ROPE_PAIR = LANES // 2
MLA_X1_LANE = ROPE_PAIR - MLA_ROPE // 2


def _rope_slab(x, cos_t, sin_t):
    return x * cos_t + pltpu.roll(x, ROPE_PAIR, 1) * sin_t


TAB_FREQS = 32
TAB_COS_MLA, TAB_COS_DIFF = 0, 16
TAB_SIN_MLA, TAB_SIN_DIFF = TAB_FREQS, TAB_FREQS + 16
TAB_SLOTS = LANES // TAB_FREQS


def _rope_table_kernel(pos_ref, inv_ref, tab_ref):
    ang = pos_ref[...].astype(F32) * inv_ref[...]
    cos = jnp.cos(ang)
    sin = jnp.sin(ang)
    lane = lax.broadcasted_iota(jnp.int32, ang.shape, 1)
    for slot in range(TAB_SLOTS):
        shift_c = (LANES - slot * TAB_FREQS) % LANES
        shift_s = (LANES + TAB_FREQS - slot * TAB_FREQS) % LANES
        c = cos if shift_c == 0 else pltpu.roll(cos, shift_c, 1)
        s = sin if shift_s == 0 else pltpu.roll(sin, shift_s, 1)
        tab_ref[slot] = jnp.where(lane < TAB_FREQS, c, jnp.where(lane < 2 * TAB_FREQS, s, 0.0))


def _rope_table(positions):
    b, s = positions.shape
    rows = b * s // TAB_SLOTS
    inv_mla = ROPE_THETA ** (-jnp.arange(0, MLA_ROPE, 2, dtype=F32) / MLA_ROPE)
    inv_diff = ROPE_THETA ** (-jnp.arange(0, ROT_DIFF, 2, dtype=F32) / ROT_DIFF)
    inv = jnp.concatenate([inv_mla, inv_diff, inv_diff])
    inv = jnp.tile(inv, TAB_SLOTS).reshape(1, LANES)
    pos = jnp.broadcast_to(positions.reshape(TAB_SLOTS, rows).T[:, :, None], (rows, TAB_SLOTS, TAB_FREQS))
    pos = pos.reshape(rows, LANES)
    blk = ROPE_ROWS
    tab = pl.pallas_call(
        _rope_table_kernel,
        grid=(rows // blk,),
        in_specs=[pl.BlockSpec((blk, LANES), lambda i: (i, 0)), _const_spec((1, LANES))],
        out_specs=pl.BlockSpec((TAB_SLOTS, blk, LANES), lambda i: (0, i, 0)),
        out_shape=jax.ShapeDtypeStruct((TAB_SLOTS, rows, LANES), F32),
        name="rope_table",
    )(pos, inv)
    return tab.reshape(b, s, LANES)


def _lane_pattern(tab, pieces, default):
    lane = lax.broadcasted_iota(jnp.int32, (1, LANES), 1)
    out = jnp.full_like(tab, default)
    for lo, hi, src, sign in pieces:
        rolled = pltpu.roll(tab, (lo - src) % LANES, 1)
        out = jnp.where((lane >= lo) & (lane < hi), -rolled if sign < 0 else rolled, out)
    return out


def _rope_patterns(tab, cos_src, sin_src, first, width):
    lo2 = first + ROPE_PAIR
    cos_t = _lane_pattern(tab, [(first, first + width, cos_src, 1), (lo2, lo2 + width, cos_src, 1)], 1.0)
    sin_t = _lane_pattern(tab, [(first, first + width, sin_src, -1), (lo2, lo2 + width, sin_src, 1)], 0.0)
    return cos_t, sin_t


def _mod_kernel(c_ref, w_ref, b_ref, o_ref):
    c = c_ref[...]
    cond = (c * jax.nn.sigmoid(c)).astype(BF16)
    o_ref[0] = jnp.dot(cond, w_ref[0].astype(BF16), preferred_element_type=F32) + b_ref[0]


def _modulation(c, ada_w, ada_b):
    depth, d, n = ada_w.shape
    b = c.shape[0]
    blk = MOD_COLS
    mod = pl.pallas_call(
        _mod_kernel,
        grid=(depth, n // blk),
        in_specs=[
            _const_spec((b, d)),
            pl.BlockSpec((1, d, blk), lambda i, j: (i, 0, j)),
            pl.BlockSpec((1, 1, blk), lambda i, j: (i, 0, j)),
        ],
        out_specs=pl.BlockSpec((1, b, blk), lambda i, j: (i, 0, j)),
        out_shape=jax.ShapeDtypeStruct((depth, b, n), F32),
        compiler_params=pltpu.CompilerParams(vmem_limit_bytes=VMEM_LIMIT),
        name="adaln_mod",
    )(c, ada_w, ada_b.reshape(depth, 1, n))
    return mod.reshape(depth, b, 6, d)


def _l0_pre_kernel(x_ref, mod_ref, n1g_ref, win_ref, qg_ref, kvg_ref, wq_ref, wkv_ref, tab_ref,
                   pw_ref, pb_ref, ps_ref, q_out, k_out, v_out, pool_out, halo_sc, *, q_scale):
    si = pl.program_id(1)
    t = x_ref.shape[1]

    @pl.when(si == 0)
    def _():
        halo_sc[...] = jnp.zeros_like(halo_sc)

    mod = mod_ref[0]
    h = _rms(x_ref[0], n1g_ref[...]) * (1.0 + mod[1:2]) + mod[0:1]
    h = h.astype(BF16)
    n_lat = Q_LORA + KV_LORA + LANES
    proj = jnp.dot(h, win_ref[:, :n_lat], preferred_element_type=F32)
    u = jnp.dot(h, win_ref[:, n_lat:], preferred_element_type=F32)
    c_q = proj[:, :Q_LORA]
    c_kv = proj[:, Q_LORA:Q_LORA + KV_LORA]
    k_rope = proj[:, Q_LORA + KV_LORA:]

    cos_t, sin_t = _rope_patterns(tab_ref[0], TAB_COS_MLA, TAB_SIN_MLA, MLA_X1_LANE, MLA_ROPE // 2)

    q = jnp.dot(_rms(c_q, qg_ref[...]).astype(BF16), wq_ref[...], preferred_element_type=F32)
    for hd in range(MLA_HEADS):
        sl = slice(hd * LANES, (hd + 1) * LANES)
        q_out[0, :, sl] = (_rope_slab(q[:, sl], cos_t, sin_t) * q_scale).astype(BF16)

    kv = jnp.dot(_rms(c_kv, kvg_ref[...]).astype(BF16), wkv_ref[...], preferred_element_type=F32)
    k_rope = _rope_slab(k_rope, cos_t, sin_t)
    for hd in range(MLA_HEADS):
        sl = slice(hd * LANES, (hd + 1) * LANES)
        k_out[0, :, sl] = (kv[:, sl] + k_rope).astype(BF16)
    v_out[0] = kv[:, MLA_HEADS * LANES:].astype(BF16)

    ext = jnp.concatenate([halo_sc[...], u], axis=0)
    halo_sc[...] = u[t - POOL_HALO:, :]
    tok = si * t + lax.broadcasted_iota(jnp.int32, (t, 1), 0)
    for g, win in enumerate(POOL_WINDOWS):
        sl = slice(g * POOL_GROUP, (g + 1) * POOL_GROUP)
        e = ext[:, sl]
        shift = 1
        while shift < win:
            e = e + pltpu.roll(e, shift, 0)
            shift *= 2
        cnt = jnp.minimum(tok + 1, win).astype(F32)
        pooled = e[POOL_HALO:] * (1.0 / cnt) - u[:, sl]
        y = jnp.dot(pooled.astype(BF16), pw_ref[g], preferred_element_type=F32) + pb_ref[:, sl]
        pool_out[0, :, sl] = (y * ps_ref[:, sl]).astype(BF16)


def _l0_pre(x, mod, n1g, w_in_ext, qg, kvg, wq, wkv, rope_tab, pw, pb, ps):
    b, s, d = x.shape
    t = TOK_TILE
    tok = lambda w: pl.BlockSpec((1, t, w), lambda i, j: (i, j, 0))
    q_scale = (MLA_NOPE + MLA_ROPE) ** -0.5 * LOG2E
    return pl.pallas_call(
        functools.partial(_l0_pre_kernel, q_scale=q_scale),
        grid=(b, s // t),
        in_specs=[
            tok(d),
            pl.BlockSpec((1, 6, d), lambda i, j: (i, 0, 0)),
            _const_spec(n1g.shape), _const_spec(w_in_ext.shape), _const_spec(qg.shape), _const_spec(kvg.shape),
            _const_spec(wq.shape), _const_spec(wkv.shape),
            tok(LANES),
            _const_spec(pw.shape), _const_spec(pb.shape), _const_spec(ps.shape),
        ],
        out_specs=[tok(MLA_HEADS * LANES), tok(MLA_HEADS * LANES), tok(MLA_HEADS * MLA_V), tok(POOL_WIDTH)],
        out_shape=[
            jax.ShapeDtypeStruct((b, s, MLA_HEADS * LANES), BF16),
            jax.ShapeDtypeStruct((b, s, MLA_HEADS * LANES), BF16),
            jax.ShapeDtypeStruct((b, s, MLA_HEADS * MLA_V), BF16),
            jax.ShapeDtypeStruct((b, s, POOL_WIDTH), BF16),
        ],
        scratch_shapes=[pltpu.VMEM((POOL_HALO, POOL_WIDTH), F32)],
        compiler_params=pltpu.CompilerParams(
            dimension_semantics=("arbitrary", "arbitrary"), vmem_limit_bytes=VMEM_LIMIT),
        name="l0_pre",
    )(x, mod, n1g, w_in_ext, qg, kvg, wq, wkv, rope_tab, pw, pb, ps)


def _l1_pre_kernel(x_ref, mod_ref, n1g_ref, wqkv_ref, tab_ref, q_out, k_out, v_out, *, q_scale):
    t = x_ref.shape[1]
    mod = mod_ref[0]
    h = (_rms(x_ref[0], n1g_ref[...]) * (1.0 + mod[1:2]) + mod[0:1]).astype(BF16)
    cos_t, sin_t = _rope_patterns(tab_ref[0], TAB_COS_DIFF, TAB_SIN_DIFF, 0, ROT_DIFF)
    width = DIFF_HEADS * LANES
    q = jnp.dot(h, wqkv_ref[:, :width], preferred_element_type=F32)
    for hd in range(DIFF_HEADS):
        sl = slice(hd * LANES, (hd + 1) * LANES)
        q_out[0, :, sl] = (_rope_slab(q[:, sl], cos_t, sin_t) * q_scale).astype(BF16)
    k = jnp.dot(h, wqkv_ref[:, width:2 * width], preferred_element_type=F32)
    for hd in range(DIFF_HEADS):
        sl = slice(hd * LANES, (hd + 1) * LANES)
        k_out[0, :, sl] = _rope_slab(k[:, sl], cos_t, sin_t).astype(BF16)
    v_out[0] = jnp.dot(h, wqkv_ref[:, 2 * width:], preferred_element_type=F32).astype(BF16)


def _l1_pre(x, mod, n1g, wqkv, rope_tab):
    b, s, d = x.shape
    t = TOK_TILE
    tok = lambda w: pl.BlockSpec((1, t, w), lambda i, j: (i, j, 0))
    width = DIFF_HEADS * LANES
    return pl.pallas_call(
        functools.partial(_l1_pre_kernel, q_scale=DIFF_HD ** -0.5 * LOG2E),
        grid=(b, s // t),
        in_specs=[
            tok(d),
            pl.BlockSpec((1, 6, d), lambda i, j: (i, 0, 0)),
            _const_spec(n1g.shape), _const_spec(wqkv.shape),
            tok(LANES),
        ],
        out_specs=[tok(width)] * 3,
        out_shape=[jax.ShapeDtypeStruct((b, s, width), BF16)] * 3,
        compiler_params=pltpu.CompilerParams(
            dimension_semantics=("arbitrary", "arbitrary"), vmem_limit_bytes=VMEM_LIMIT),
        name="l1_pre",
    )(x, mod, n1g, wqkv, rope_tab)


def _flash_group(qs, k_ats, v_ats, n_full, row_pos, m_sc, acc_sc, p_sc, alpha_sc):
    n_grp = len(qs)
    n_blk = ATTN_TK // LANES
    acc_blk = acc_sc.shape[-1] // LANES
    col = n_full * ATTN_TK + lax.broadcasted_iota(jnp.int32, (qs[0].shape[0], ATTN_TK), 1)
    mask_bias = jnp.where(col <= row_pos, 0.0, MASK_VALUE)

    def scores(g, j, first):
        s = lax.dot_general(qs[g], k_ats[g](j), (((1,), (1,)), ((), ())), preferred_element_type=F32)
        if first:
            s = s + mask_bias
        blocks = [s[:, c * LANES:(c + 1) * LANES] for c in range(n_blk)]
        m_blk = blocks[0]
        for blk in blocks[1:]:
            m_blk = jnp.maximum(m_blk, blk)
        m_new = jnp.max(m_blk, axis=1, keepdims=True)
        if first:
            m_new = jnp.broadcast_to(m_new, m_sc.shape[1:])
        else:
            m_prev = m_sc[g]
            m_new = jnp.maximum(m_prev, m_new)
            alpha_sc[g] = jnp.exp2(m_prev - m_new)
        p_sc[g] = jnp.concatenate([jnp.exp2(blk - m_new).astype(BF16) for blk in blocks], axis=1)
        m_sc[g] = m_new

    def values(g, j):
        pv = jnp.dot(p_sc[g], v_ats[g](j), preferred_element_type=F32)
        alpha = alpha_sc[g]
        alpha_w = alpha if acc_blk == 1 else jnp.concatenate([alpha] * acc_blk, axis=1)
        acc_sc[g] = alpha_w * acc_sc[g] + pv

    for g in range(n_grp):
        scores(g, n_full, True)

    def trip(j):
        prev = jnp.where(j == 0, n_full, j - 1)
        for g in range(n_grp):
            values(g, prev)
            scores(g, j, False)

    def two_trips(t, carry):
        trip(2 * t)
        trip(2 * t + 1)
        return carry

    def odd_trip(_, carry):
        trip(n_full - 1)
        return carry

    lax.fori_loop(0, lax.shift_right_logical(n_full, 1), two_trips, 0)
    lax.fori_loop(0, n_full & 1, odd_trip, 0)
    last = jnp.maximum(n_full - 1, 0)
    for g in range(n_grp):
        values(g, last)


def _flash_scratch(rows, acc_width):
    return [pltpu.VMEM((ATTN_GROUP, rows, LANES), F32), pltpu.VMEM((ATTN_GROUP, rows, acc_width), F32),
            pltpu.VMEM((ATTN_GROUP, rows, ATTN_TK), BF16), pltpu.VMEM((ATTN_GROUP, rows, LANES), F32)]


def _zero_at_first_step(acc_sc, alpha_sc):
    @pl.when((pl.program_id(0) == 0) & (pl.program_id(1) == 0))
    def _():
        def zero(g, carry):
            acc_sc[g] = jnp.zeros(acc_sc.shape[1:], F32)
            alpha_sc[g] = jnp.zeros(alpha_sc.shape[1:], F32)
            return carry

        lax.fori_loop(0, acc_sc.shape[0], zero, 0)


def _key_step(ref, lanes):
    return lambda j: ref[0, pl.ds(pl.multiple_of(j * ATTN_TK, ATTN_TK), ATTN_TK), lanes]


def _mla_attn_kernel(q_ref, k_ref, v_ref, o_ref, m_sc, acc_sc, p_sc, alpha_sc):
    tq = q_ref.shape[1]
    qi = pl.program_id(1)
    n_full = (qi * tq) // ATTN_TK
    row_pos = qi * tq + lax.broadcasted_iota(jnp.int32, (tq, 1), 0)
    lane = lax.broadcasted_iota(jnp.int32, (tq, LANES), 1)
    one = jnp.ones((ATTN_TK, LANES), BF16)
    n_grp = m_sc.shape[0]
    _zero_at_first_step(acc_sc, alpha_sc)

    def value_step(hd):
        raw = _key_step(v_ref, slice((hd // 2) * LANES, (hd // 2 + 1) * LANES))
        return lambda j: jnp.concatenate([raw(j), one], axis=1)

    for first in range(0, MLA_HEADS, n_grp):
        heads = range(first, first + n_grp)
        slabs = [slice(hd * LANES, (hd + 1) * LANES) for hd in heads]
        _flash_group([q_ref[0, :, sl] for sl in slabs], [_key_step(k_ref, sl) for sl in slabs],
                     [value_step(hd) for hd in heads], n_full, row_pos, m_sc, acc_sc, p_sc, alpha_sc)
        for g in range(0, n_grp, 2):
            even = acc_sc[g, :, :LANES] / acc_sc[g, :, LANES:]
            odd = acc_sc[g + 1, :, :LANES] / acc_sc[g + 1, :, LANES:]
            pair = (first + g) // 2
            o_ref[0, :, pair * LANES:(pair + 1) * LANES] = jnp.where(lane < MLA_V, even, odd).astype(BF16)
            acc_sc[g] = jnp.zeros(acc_sc.shape[1:], F32)
            acc_sc[g + 1] = jnp.zeros(acc_sc.shape[1:], F32)


def _mla_attention(q, k, v):
    b, s, _ = q.shape
    tq = MLA_TQ
    return pl.pallas_call(
        _mla_attn_kernel,
        grid=(b, s // tq),
        in_specs=[
            pl.BlockSpec((1, tq, q.shape[2]), lambda i, j: (i, j, 0)),
            pl.BlockSpec((1, s, k.shape[2]), lambda i, j: (i, 0, 0)),
            pl.BlockSpec((1, s, v.shape[2]), lambda i, j: (i, 0, 0)),
        ],
        out_specs=pl.BlockSpec((1, tq, v.shape[2]), lambda i, j: (i, j, 0)),
        out_shape=jax.ShapeDtypeStruct(v.shape, BF16),
        scratch_shapes=_flash_scratch(tq, 2 * LANES),
        compiler_params=pltpu.CompilerParams(
            dimension_semantics=("arbitrary", "arbitrary"), vmem_limit_bytes=VMEM_LIMIT),
        name="mla_attention",
    )(q, k, v)


def _diff_attn_kernel(q_ref, k_ref, v_ref, lam_ref, g_ref, o_ref, m_sc, acc_sc, p_sc, alpha_sc, *, lambda_init):
    tq = q_ref.shape[1]
    qi = pl.program_id(1)
    n_full = (qi * tq) // ATTN_TK
    row = lax.broadcasted_iota(jnp.int32, (2 * tq, 1), 0)
    row_pos = qi * tq + jnp.where(row >= tq, row - tq, row)
    lane = lax.broadcasted_iota(jnp.int32, (tq, LANES), 1)
    half = ROT_DIFF // 2
    comp0 = (lane < half) | ((lane >= ROT_DIFF) & (lane < DIFF_HD + half))
    one = jnp.ones((ATTN_TK, LANES), BF16)
    _zero_at_first_step(acc_sc, alpha_sc)
    lv = lam_ref[...]
    lam = (jnp.exp(jnp.sum(lv[0:1] * lv[1:2], axis=1, keepdims=True))
           - jnp.exp(jnp.sum(lv[2:3] * lv[3:4], axis=1, keepdims=True)) + lambda_init)
    n_grp = m_sc.shape[0]
    for first in range(0, DIFF_HEADS, n_grp):
        slabs = [slice(hd * LANES, (hd + 1) * LANES) for hd in range(first, first + n_grp)]
        qs = []
        for sl in slabs:
            qh = q_ref[0, :, sl]
            zero = jnp.zeros_like(qh)
            qs.append(jnp.concatenate(
                [jnp.where(comp0, qh, zero), jnp.where(comp0, zero, qh)], axis=0))
        v_ats = [lambda j, raw=_key_step(v_ref, sl): jnp.concatenate([raw(j), one], axis=1) for sl in slabs]
        _flash_group(qs, [_key_step(k_ref, sl) for sl in slabs], v_ats, n_full, row_pos,
                     m_sc, acc_sc, p_sc, alpha_sc)
        for g, sl in enumerate(slabs):
            o = acc_sc[g, :, :LANES] / acc_sc[g, :, LANES:]
            o = o[:tq] - lam * o[tq:]
            o_ref[0, :, sl] = (_rms(o, g_ref[...]) * (1.0 - lambda_init)).astype(BF16)
            acc_sc[g] = jnp.zeros(acc_sc.shape[1:], F32)


def _diff_attention(q, k, v, lam_vecs, subln_g, lambda_init):
    b, s, w = q.shape
    tq = DIFF_TQ
    return pl.pallas_call(
        functools.partial(_diff_attn_kernel, lambda_init=lambda_init),
        grid=(b, s // tq),
        in_specs=[
            pl.BlockSpec((1, tq, w), lambda i, j: (i, j, 0)),
            pl.BlockSpec((1, s, w), lambda i, j: (i, 0, 0)),
            pl.BlockSpec((1, s, w), lambda i, j: (i, 0, 0)),
            _const_spec(lam_vecs.shape), _const_spec(subln_g.shape),
        ],
        out_specs=pl.BlockSpec((1, tq, w), lambda i, j: (i, j, 0)),
        out_shape=jax.ShapeDtypeStruct((b, s, w), BF16),
        scratch_shapes=_flash_scratch(2 * tq, 2 * LANES),
        compiler_params=pltpu.CompilerParams(
            dimension_semantics=("arbitrary", "arbitrary"), vmem_limit_bytes=VMEM_LIMIT),
        name="diff_attention",
    )(q, k, v, lam_vecs, subln_g)


def _post_kernel(*refs, n_mix, final_norm):
    mix_refs = refs[:n_mix]
    x_ref, mod_ref, wout_ref, n2g_ref, wgu_ref, wd_ref = refs[n_mix:n_mix + 6]
    rest = refs[n_mix + 6:]
    if final_norm:
        fng_ref, o_ref, acc_sc = rest
    else:
        o_ref, acc_sc = rest
    mod = mod_ref[0]
    y = None
    row = 0
    for r in mix_refs:
        w = r.shape[2]
        part = jnp.dot(r[0], wout_ref[row:row + w, :], preferred_element_type=F32)
        y = part if y is None else y + part
        row += w
    x1 = x_ref[0] + mod[2:3] * y
    h = (_rms(x1, n2g_ref[...]) * (1.0 + mod[4:5]) + mod[3:4]).astype(BF16)

    for c in range(N_FF_CHUNKS):
        cols = slice(c * FF_CHUNK, (c + 1) * FF_CHUNK)
        ucols = slice(D_FF + c * FF_CHUNK, D_FF + (c + 1) * FF_CHUNK)
        g = jnp.dot(h, wgu_ref[0, :, cols], preferred_element_type=F32)
        u = jnp.dot(h, wgu_ref[0, :, ucols], preferred_element_type=F32)
        act = (g * jax.nn.sigmoid(g) * u).astype(BF16)
        down = jnp.dot(act, wd_ref[0, cols, :], preferred_element_type=F32)
        if c == 0:
            acc_sc[...] = down
        else:
            acc_sc[...] += down
    x2 = x1 + mod[5:6] * acc_sc[...]
    if final_norm:
        x2 = _rms(x2, fng_ref[...])
    o_ref[0] = x2


def _post(mix_inputs, x, mod, wout, n2g, wgu, wd, layer, final_g=None):
    b, s, d = x.shape
    t = POST_TILE
    tok = lambda w: pl.BlockSpec((1, t, w), lambda i, j: (i, j, 0))
    layer_spec = lambda a: pl.BlockSpec((1,) + a.shape[1:], lambda i, j: (layer, 0, 0),
                                        pipeline_mode=pl.Buffered(1))
    final_norm = final_g is not None
    in_specs = [tok(m.shape[2]) for m in mix_inputs] + [
        tok(d),
        pl.BlockSpec((1, 6, d), lambda i, j: (i, 0, 0)),
        _const_spec(wout.shape), _const_spec(n2g.shape), layer_spec(wgu), layer_spec(wd),
    ]
    args = list(mix_inputs) + [x, mod, wout, n2g, wgu, wd]
    if final_norm:
        in_specs.append(_const_spec(final_g.shape))
        args.append(final_g)
    return pl.pallas_call(
        functools.partial(_post_kernel, n_mix=len(mix_inputs), final_norm=final_norm),
        grid=(b, s // t),
        in_specs=in_specs,
        out_specs=tok(d),
        out_shape=jax.ShapeDtypeStruct((b, s, d), F32),
        scratch_shapes=[pltpu.VMEM((t, d), F32)],
        compiler_params=pltpu.CompilerParams(
            dimension_semantics=("arbitrary", "arbitrary"), vmem_limit_bytes=VMEM_LIMIT),
        name="post_final" if final_norm else "post",
    )(*args)


def _mla_slab(nope, rope):
    split = MLA_X1_LANE
    half = MLA_ROPE // 2
    pad = jnp.zeros(nope.shape[:-1] + (LANES - MLA_NOPE - MLA_ROPE,), nope.dtype)
    return jnp.concatenate([nope[..., :split], rope[..., :half], nope[..., split:], pad, rope[..., half:]], axis=-1)


def _diff_qkv_weights(w_qkv):
    half = ROT_DIFF // 2
    c0, c1 = 0, DIFF_HD
    order = (list(range(c0, c0 + half)) + list(range(c1, c1 + half)) + list(range(c0 + ROT_DIFF, c0 + DIFF_HD))
             + list(range(c0 + half, c0 + ROT_DIFF)) + list(range(c1 + half, c1 + ROT_DIFF))
             + list(range(c1 + ROT_DIFF, c1 + DIFF_HD)))
    assert sorted(order) == list(range(LANES))
    width = DIFF_HEADS * LANES
    head_perm = [hd * LANES + o for hd in range(DIFF_HEADS) for o in order]
    perm = jnp.asarray(head_perm + [width + p for p in head_perm] + list(range(2 * width, 3 * width)), jnp.int32)
    return w_qkv[:, perm].astype(BF16)


def kernel(x, c, positions, ada_w, ada_b, norm1_g, norm2_g, ffn_w_gate_up, ffn_w_down, mla_w_in, mla_q_norm_g,
           mla_kv_norm_g, mla_w_uq, mla_w_ukv, pool_w, pool_b, pool_scale, mix_a_w_out, diff_w_qkv,
           diff_lambda_q1, diff_lambda_k1, diff_lambda_q2, diff_lambda_k2, diff_subln_g, diff_w_out,
           final_norm_g):
    d = D_MODEL
    rope_tab = _rope_table(positions)
    mod = _modulation(c, ada_w, ada_b)

    w_in = mla_w_in[0]
    o_kv, o_kr, o_u = Q_LORA, Q_LORA + KV_LORA, Q_LORA + KV_LORA + MLA_ROPE
    w_kr = _mla_slab(jnp.zeros((d, MLA_NOPE), F32), w_in[:, o_kr:o_u])
    w_in_ext = jnp.concatenate([w_in[:, :o_kr], w_kr, w_in[:, o_u:]], axis=1).astype(BF16)
    w_uq = mla_w_uq[0].reshape(Q_LORA, MLA_HEADS, MLA_NOPE + MLA_ROPE)
    wq = _mla_slab(w_uq[..., :MLA_NOPE], w_uq[..., MLA_NOPE:]).reshape(Q_LORA, MLA_HEADS * LANES).astype(BF16)
    w_ukv = mla_w_ukv[0].reshape(KV_LORA, MLA_HEADS, MLA_NOPE + MLA_V)
    wk = _mla_slab(w_ukv[..., :MLA_NOPE], jnp.zeros((KV_LORA, MLA_HEADS, MLA_ROPE), F32))
    wk = wk.reshape(KV_LORA, MLA_HEADS * LANES)
    wv = w_ukv[..., MLA_NOPE:].reshape(KV_LORA, MLA_HEADS * MLA_V)
    wkv = jnp.concatenate([wk, wv], axis=1).astype(BF16)
    q0, k0, v0, pool = _l0_pre(
        x, mod[0], norm1_g[0:1], w_in_ext, mla_q_norm_g, mla_kv_norm_g, wq, wkv, rope_tab,
        pool_w[0].astype(BF16), pool_b[0].reshape(1, POOL_WIDTH), pool_scale)
    attn0 = _mla_attention(q0, k0, v0)
    wgu, wd = ffn_w_gate_up.astype(BF16), ffn_w_down.astype(BF16)
    x = _post([attn0, pool], x, mod[0], mix_a_w_out[0].astype(BF16), norm2_g[0:1], wgu, wd, 0)

    lambda_init = 0.8 - 0.6 * math.exp(-0.3 * 1)
    q1, k1, v1 = _l1_pre(x, mod[1], norm1_g[1:2], _diff_qkv_weights(diff_w_qkv[0]), rope_tab)
    lam_vecs = jnp.concatenate([diff_lambda_q1, diff_lambda_k1, diff_lambda_q2, diff_lambda_k2], axis=0)
    attn1 = _diff_attention(q1, k1, v1, lam_vecs, diff_subln_g, lambda_init)
    return _post([attn1], x, mod[1], diff_w_out[0].astype(BF16), norm2_g[1:2], wgu, wd, 1,
                 final_g=final_norm_g.reshape(1, d))
```

```python
import functools
import math

import jax
import jax.numpy as jnp
from jax import lax
from jax.experimental import pallas as pl
from jax.experimental.pallas import tpu as pltpu

D_MODEL = 1024
EPS = 1e-6
ROPE_THETA = 500000.0
MLA_HEADS = 8
MLA_NOPE = 64
MLA_ROPE = 32
MLA_V = 64
Q_LORA = 384
KV_LORA = 256
POOL_WINDOWS = (2, 4, 8, 16)
POOL_GROUP = 128
POOL_WIDTH = POOL_GROUP * len(POOL_WINDOWS)
DIFF_HEADS = 8
DIFF_HD = 64
ROT_DIFF = DIFF_HD // 4
D_FF = 2816

LANES = 128
POOL_HALO = 16
FF_CHUNK = 256
N_FF_CHUNKS = D_FF // FF_CHUNK
assert N_FF_CHUNKS * FF_CHUNK == D_FF

TOK_TILE = 1024
POST_TILE = 1024
MLA_TQ = 512
DIFF_TQ = 256
ATTN_TK = 512
ATTN_GROUP = 8
VMEM_LIMIT = 56 * 1024 * 1024

LOG2E = 1.4426950408889634
MASK_VALUE = -1e30

F32 = jnp.float32
BF16 = jnp.bfloat16


def _const_spec(shape):
    nd = len(shape)
    return pl.BlockSpec(shape, lambda *_: (0,) * nd, pipeline_mode=pl.Buffered(1))


def _rms(x, g):
    return x * lax.rsqrt(jnp.mean(x * x, axis=-1, keepdims=True) + EPS) * g


ROPE_PAIR = LANES // 2
MLA_X1_LANE = ROPE_PAIR - MLA_ROPE // 2


def _rope_slab(x, cos_t, sin_t):
    return x * cos_t + pltpu.roll(x, ROPE_PAIR, 1) * sin_t


TAB_FREQS = 32
TAB_COS_MLA, TAB_COS_DIFF = 0, 16
TAB_SIN_MLA, TAB_SIN_DIFF = TAB_FREQS, TAB_FREQS + 16
TAB_SLOTS = LANES // TAB_FREQS


def _rope_table_kernel(pos_ref, inv_ref, tab_ref):
    ang = pos_ref[...].astype(F32) * inv_ref[...]
    cos = jnp.cos(ang)
    sin = jnp.sin(ang)
    lane = lax.broadcasted_iota(jnp.int32, ang.shape, 1)
    for slot in range(TAB_SLOTS):
        shift_c = (LANES - slot * TAB_FREQS) % LANES
        shift_s = (LANES + TAB_FREQS - slot * TAB_FREQS) % LANES
        c = cos if shift_c == 0 else pltpu.roll(cos, shift_c, 1)
        s = sin if shift_s == 0 else pltpu.roll(sin, shift_s, 1)
        tab_ref[slot] = jnp.where(lane < TAB_FREQS, c, jnp.where(lane < 2 * TAB_FREQS, s, 0.0))


def _lane_pattern(tab, pieces, default):
    lane = lax.broadcasted_iota(jnp.int32, (1, LANES), 1)
    out = jnp.full_like(tab, default)
    for lo, hi, src, sign in pieces:
        rolled = pltpu.roll(tab, (lo - src) % LANES, 1)
        out = jnp.where((lane >= lo) & (lane < hi), -rolled if sign < 0 else rolled, out)
    return out


def _rope_patterns(tab, cos_src, sin_src, first, width):
    lo2 = first + ROPE_PAIR
    cos_t = _lane_pattern(tab, [(first, first + width, cos_src, 1), (lo2, lo2 + width, cos_src, 1)], 1.0)
    sin_t = _lane_pattern(tab, [(first, first + width, sin_src, -1), (lo2, lo2 + width, sin_src, 1)], 0.0)
    return cos_t, sin_t


def _mod_kernel(c_ref, w_ref, b_ref, o_ref):
    c = c_ref[...]
    cond = (c * jax.nn.sigmoid(c)).astype(BF16)
    o_ref[0] = jnp.dot(cond, w_ref[0].astype(BF16), preferred_element_type=F32) + b_ref[0]


def _setup_kernel(pos_ref, inv_ref, c_ref, w_ref, b_ref, tab_ref, mod_ref):
    _rope_table_kernel(pos_ref, inv_ref, tab_ref)
    _mod_kernel(c_ref, w_ref, b_ref, mod_ref)


def _rope_and_modulation(positions, c, ada_w, ada_b):
    b, s = positions.shape
    rows = b * s // TAB_SLOTS
    inv_mla = ROPE_THETA ** (-jnp.arange(0, MLA_ROPE, 2, dtype=F32) / MLA_ROPE)
    inv_diff = ROPE_THETA ** (-jnp.arange(0, ROT_DIFF, 2, dtype=F32) / ROT_DIFF)
    inv = jnp.tile(jnp.concatenate([inv_mla, inv_diff, inv_diff]), TAB_SLOTS).reshape(1, LANES)
    pos = jnp.broadcast_to(positions.reshape(TAB_SLOTS, rows).T[:, :, None], (rows, TAB_SLOTS, TAB_FREQS))
    pos = pos.reshape(rows, LANES)
    depth, d, n = ada_w.shape
    nb = c.shape[0]
    col_blk = 1536
    per_layer = n // col_blk
    steps = depth * per_layer
    row_blk = rows // steps
    assert row_blk * steps == rows and per_layer * col_blk == n
    w_idx = lambda i: (i // per_layer, 0, i % per_layer)
    tab, mod = pl.pallas_call(
        _setup_kernel,
        grid=(steps,),
        in_specs=[
            pl.BlockSpec((row_blk, LANES), lambda i: (i, 0)), _const_spec((1, LANES)), _const_spec((nb, d)),
            pl.BlockSpec((1, d, col_blk), w_idx), pl.BlockSpec((1, 1, col_blk), w_idx),
        ],
        out_specs=[pl.BlockSpec((TAB_SLOTS, row_blk, LANES), lambda i: (0, i, 0)),
                   pl.BlockSpec((1, nb, col_blk), w_idx)],
        out_shape=[jax.ShapeDtypeStruct((TAB_SLOTS, rows, LANES), F32), jax.ShapeDtypeStruct((depth, nb, n), F32)],
        compiler_params=pltpu.CompilerParams(dimension_semantics=("arbitrary",), vmem_limit_bytes=VMEM_LIMIT),
        name="rope_and_adaln",
    )(pos, inv, c, ada_w, ada_b.reshape(depth, 1, n))
    return tab.reshape(b, s, LANES), mod.reshape(depth, nb, 6, d)


def _l0_pre_kernel(x_ref, mod_ref, n1g_ref, win_ref, qg_ref, kvg_ref, wq_ref, wkv_ref, tab_ref,
                   pw_ref, pb_ref, ps_ref, q_out, k_out, v_out, pool_out, halo_sc, *, q_scale):
    si = pl.program_id(1)
    t = x_ref.shape[1]

    @pl.when(si == 0)
    def _():
        halo_sc[...] = jnp.zeros_like(halo_sc)

    mod = mod_ref[0]
    h = _rms(x_ref[0], n1g_ref[...]) * (1.0 + mod[1:2]) + mod[0:1]
    h = h.astype(BF16)
    n_lat = Q_LORA + KV_LORA + LANES
    proj = jnp.dot(h, win_ref[:, :n_lat], preferred_element_type=F32)
    u = jnp.dot(h, win_ref[:, n_lat:], preferred_element_type=F32)
    c_q = proj[:, :Q_LORA]
    c_kv = proj[:, Q_LORA:Q_LORA + KV_LORA]
    k_rope = proj[:, Q_LORA + KV_LORA:]

    cos_t, sin_t = _rope_patterns(tab_ref[0], TAB_COS_MLA, TAB_SIN_MLA, MLA_X1_LANE, MLA_ROPE // 2)

    q = jnp.dot(_rms(c_q, qg_ref[...]).astype(BF16), wq_ref[...], preferred_element_type=F32)
    for hd in range(MLA_HEADS):
        sl = slice(hd * LANES, (hd + 1) * LANES)
        q_out[0, :, sl] = (_rope_slab(q[:, sl], cos_t, sin_t) * q_scale).astype(BF16)

    kv = jnp.dot(_rms(c_kv, kvg_ref[...]).astype(BF16), wkv_ref[...], preferred_element_type=F32)
    k_rope = _rope_slab(k_rope, cos_t, sin_t)
    for hd in range(MLA_HEADS):
        sl = slice(hd * LANES, (hd + 1) * LANES)
        k_out[0, :, sl] = (kv[:, sl] + k_rope).astype(BF16)
    v_out[0] = kv[:, MLA_HEADS * LANES:].astype(BF16)

    ext = jnp.concatenate([halo_sc[...], u], axis=0)
    halo_sc[...] = u[t - POOL_HALO:, :]
    tok = si * t + lax.broadcasted_iota(jnp.int32, (t, 1), 0)
    for g, win in enumerate(POOL_WINDOWS):
        sl = slice(g * POOL_GROUP, (g + 1) * POOL_GROUP)
        e = ext[:, sl]
        shift = 1
        while shift < win:
            e = e + pltpu.roll(e, shift, 0)
            shift *= 2
        cnt = jnp.minimum(tok + 1, win).astype(F32)
        pooled = e[POOL_HALO:] * (1.0 / cnt) - u[:, sl]
        y = jnp.dot(pooled.astype(BF16), pw_ref[g], preferred_element_type=F32) + pb_ref[:, sl]
        pool_out[0, :, sl] = (y * ps_ref[:, sl]).astype(BF16)


def _l0_pre(x, mod, n1g, w_in_ext, qg, kvg, wq, wkv, rope_tab, pw, pb, ps):
    b, s, d = x.shape
    t = TOK_TILE
    tok = lambda w: pl.BlockSpec((1, t, w), lambda i, j: (i, j, 0))
    q_scale = (MLA_NOPE + MLA_ROPE) ** -0.5 * LOG2E
    return pl.pallas_call(
        functools.partial(_l0_pre_kernel, q_scale=q_scale),
        grid=(b, s // t),
        in_specs=[
            tok(d),
            pl.BlockSpec((1, 6, d), lambda i, j: (i, 0, 0)),
            _const_spec(n1g.shape), _const_spec(w_in_ext.shape), _const_spec(qg.shape), _const_spec(kvg.shape),
            _const_spec(wq.shape), _const_spec(wkv.shape),
            tok(LANES),
            _const_spec(pw.shape), _const_spec(pb.shape), _const_spec(ps.shape),
        ],
        out_specs=[tok(MLA_HEADS * LANES), tok(MLA_HEADS * LANES), tok(MLA_HEADS * MLA_V), tok(POOL_WIDTH)],
        out_shape=[
            jax.ShapeDtypeStruct((b, s, MLA_HEADS * LANES), BF16),
            jax.ShapeDtypeStruct((b, s, MLA_HEADS * LANES), BF16),
            jax.ShapeDtypeStruct((b, s, MLA_HEADS * MLA_V), BF16),
            jax.ShapeDtypeStruct((b, s, POOL_WIDTH), BF16),
        ],
        scratch_shapes=[pltpu.VMEM((POOL_HALO, POOL_WIDTH), F32)],
        compiler_params=pltpu.CompilerParams(
            dimension_semantics=("arbitrary", "arbitrary"), vmem_limit_bytes=VMEM_LIMIT),
        name="l0_pre",
    )(x, mod, n1g, w_in_ext, qg, kvg, wq, wkv, rope_tab, pw, pb, ps)


def _l1_pre_kernel(x_ref, mod_ref, n1g_ref, wqkv_ref, tab_ref, q_out, k_out, v_out, *, q_scale):
    t = x_ref.shape[1]
    mod = mod_ref[0]
    h = (_rms(x_ref[0], n1g_ref[...]) * (1.0 + mod[1:2]) + mod[0:1]).astype(BF16)
    cos_t, sin_t = _rope_patterns(tab_ref[0], TAB_COS_DIFF, TAB_SIN_DIFF, 0, ROT_DIFF)
    width = DIFF_HEADS * LANES
    q = jnp.dot(h, wqkv_ref[:, :width], preferred_element_type=F32)
    for hd in range(DIFF_HEADS):
        sl = slice(hd * LANES, (hd + 1) * LANES)
        q_out[0, :, sl] = (_rope_slab(q[:, sl], cos_t, sin_t) * q_scale).astype(BF16)
    k = jnp.dot(h, wqkv_ref[:, width:2 * width], preferred_element_type=F32)
    for hd in range(DIFF_HEADS):
        sl = slice(hd * LANES, (hd + 1) * LANES)
        k_out[0, :, sl] = _rope_slab(k[:, sl], cos_t, sin_t).astype(BF16)
    v_out[0] = jnp.dot(h, wqkv_ref[:, 2 * width:], preferred_element_type=F32).astype(BF16)


def _l1_pre(x, mod, n1g, wqkv, rope_tab):
    b, s, d = x.shape
    t = TOK_TILE
    tok = lambda w: pl.BlockSpec((1, t, w), lambda i, j: (i, j, 0))
    width = DIFF_HEADS * LANES
    return pl.pallas_call(
        functools.partial(_l1_pre_kernel, q_scale=DIFF_HD ** -0.5 * LOG2E),
        grid=(b, s // t),
        in_specs=[
            tok(d),
            pl.BlockSpec((1, 6, d), lambda i, j: (i, 0, 0)),
            _const_spec(n1g.shape), _const_spec(wqkv.shape),
            tok(LANES),
        ],
        out_specs=[tok(width)] * 3,
        out_shape=[jax.ShapeDtypeStruct((b, s, width), BF16)] * 3,
        compiler_params=pltpu.CompilerParams(
            dimension_semantics=("arbitrary", "arbitrary"), vmem_limit_bytes=VMEM_LIMIT),
        name="l1_pre",
    )(x, mod, n1g, wqkv, rope_tab)


def _flash_group(qs, k_ats, v_ats, n_full, row_pos, m_sc, acc_sc, p_sc, alpha_sc):
    n_grp = len(qs)
    n_blk = ATTN_TK // LANES
    acc_blk = acc_sc.shape[-1] // LANES
    col = n_full * ATTN_TK + lax.broadcasted_iota(jnp.int32, (qs[0].shape[0], ATTN_TK), 1)
    mask_bias = jnp.where(col <= row_pos, 0.0, MASK_VALUE)

    def scores(g, j, first):
        s = lax.dot_general(qs[g], k_ats[g](j), (((1,), (1,)), ((), ())), preferred_element_type=F32)
        if first:
            s = s + mask_bias
        blocks = [s[:, c * LANES:(c + 1) * LANES] for c in range(n_blk)]
        m_blk = blocks[0]
        for blk in blocks[1:]:
            m_blk = jnp.maximum(m_blk, blk)
        m_new = jnp.max(m_blk, axis=1, keepdims=True)
        if first:
            m_new = jnp.broadcast_to(m_new, m_sc.shape[1:])
        else:
            m_prev = m_sc[g]
            m_new = jnp.maximum(m_prev, m_new)
            alpha_sc[g] = jnp.exp2(m_prev - m_new)
        p_sc[g] = jnp.concatenate([jnp.exp2(blk - m_new).astype(BF16) for blk in blocks], axis=1)
        m_sc[g] = m_new

    def values(g, j):
        pv = jnp.dot(p_sc[g], v_ats[g](j), preferred_element_type=F32)
        alpha = alpha_sc[g]
        alpha_w = alpha if acc_blk == 1 else jnp.concatenate([alpha] * acc_blk, axis=1)
        acc_sc[g] = alpha_w * acc_sc[g] + pv

    for g in range(n_grp):
        scores(g, n_full, True)

    def trip(j):
        prev = jnp.where(j == 0, n_full, j - 1)
        for g in range(n_grp):
            values(g, prev)
            scores(g, j, False)

    def two_trips(t, carry):
        trip(2 * t)
        trip(2 * t + 1)
        return carry

    def odd_trip(_, carry):
        trip(n_full - 1)
        return carry

    lax.fori_loop(0, lax.shift_right_logical(n_full, 1), two_trips, 0)
    lax.fori_loop(0, n_full & 1, odd_trip, 0)
    last = jnp.maximum(n_full - 1, 0)
    for g in range(n_grp):
        values(g, last)


def _flash_scratch(rows, acc_width):
    return [pltpu.VMEM((ATTN_GROUP, rows, LANES), F32), pltpu.VMEM((ATTN_GROUP, rows, acc_width), F32),
            pltpu.VMEM((ATTN_GROUP, rows, ATTN_TK), BF16), pltpu.VMEM((ATTN_GROUP, rows, LANES), F32)]


def _zero_at_first_step(acc_sc, alpha_sc):
    @pl.when((pl.program_id(0) == 0) & (pl.program_id(1) == 0))
    def _():
        def zero(g, carry):
            acc_sc[g] = jnp.zeros(acc_sc.shape[1:], F32)
            alpha_sc[g] = jnp.zeros(alpha_sc.shape[1:], F32)
            return carry

        lax.fori_loop(0, acc_sc.shape[0], zero, 0)


def _key_step(ref, lanes):
    return lambda j: ref[0, pl.ds(pl.multiple_of(j * ATTN_TK, ATTN_TK), ATTN_TK), lanes]


def _mla_attn_kernel(q_ref, k_ref, v_ref, o_ref, m_sc, acc_sc, p_sc, alpha_sc):
    tq = q_ref.shape[1]
    qi = pl.program_id(1)
    n_full = (qi * tq) // ATTN_TK
    row_pos = qi * tq + lax.broadcasted_iota(jnp.int32, (tq, 1), 0)
    lane = lax.broadcasted_iota(jnp.int32, (tq, LANES), 1)
    one = jnp.ones((ATTN_TK, LANES), BF16)
    n_grp = m_sc.shape[0]
    _zero_at_first_step(acc_sc, alpha_sc)

    def value_step(hd):
        raw = _key_step(v_ref, slice((hd // 2) * LANES, (hd // 2 + 1) * LANES))
        return lambda j: jnp.concatenate([raw(j), one], axis=1)

    for first in range(0, MLA_HEADS, n_grp):
        heads = range(first, first + n_grp)
        slabs = [slice(hd * LANES, (hd + 1) * LANES) for hd in heads]
        _flash_group([q_ref[0, :, sl] for sl in slabs], [_key_step(k_ref, sl) for sl in slabs],
                     [value_step(hd) for hd in heads], n_full, row_pos, m_sc, acc_sc, p_sc, alpha_sc)
        for g in range(0, n_grp, 2):
            even = acc_sc[g, :, :LANES] / acc_sc[g, :, LANES:]
            odd = acc_sc[g + 1, :, :LANES] / acc_sc[g + 1, :, LANES:]
            pair = (first + g) // 2
            o_ref[0, :, pair * LANES:(pair + 1) * LANES] = jnp.where(lane < MLA_V, even, odd).astype(BF16)
            acc_sc[g] = jnp.zeros(acc_sc.shape[1:], F32)
            acc_sc[g + 1] = jnp.zeros(acc_sc.shape[1:], F32)


def _mla_attention(q, k, v):
    b, s, _ = q.shape
    tq = MLA_TQ
    return pl.pallas_call(
        _mla_attn_kernel,
        grid=(b, s // tq),
        in_specs=[
            pl.BlockSpec((1, tq, q.shape[2]), lambda i, j: (i, j, 0)),
            pl.BlockSpec((1, s, k.shape[2]), lambda i, j: (i, 0, 0)),
            pl.BlockSpec((1, s, v.shape[2]), lambda i, j: (i, 0, 0)),
        ],
        out_specs=pl.BlockSpec((1, tq, v.shape[2]), lambda i, j: (i, j, 0)),
        out_shape=jax.ShapeDtypeStruct(v.shape, BF16),
        scratch_shapes=_flash_scratch(tq, 2 * LANES),
        compiler_params=pltpu.CompilerParams(
            dimension_semantics=("arbitrary", "arbitrary"), vmem_limit_bytes=VMEM_LIMIT),
        name="mla_attention",
    )(q, k, v)


def _diff_attn_kernel(q_ref, k_ref, v_ref, lam_ref, g_ref, o_ref, m_sc, acc_sc, p_sc, alpha_sc, *, lambda_init):
    tq = q_ref.shape[1]
    qi = pl.program_id(1)
    n_full = (qi * tq) // ATTN_TK
    row = lax.broadcasted_iota(jnp.int32, (2 * tq, 1), 0)
    row_pos = qi * tq + jnp.where(row >= tq, row - tq, row)
    lane = lax.broadcasted_iota(jnp.int32, (tq, LANES), 1)
    half = ROT_DIFF // 2
    comp0 = (lane < half) | ((lane >= ROT_DIFF) & (lane < DIFF_HD + half))
    one = jnp.ones((ATTN_TK, LANES), BF16)
    _zero_at_first_step(acc_sc, alpha_sc)
    lv = lam_ref[...]
    lam = (jnp.exp(jnp.sum(lv[0:1] * lv[1:2], axis=1, keepdims=True))
           - jnp.exp(jnp.sum(lv[2:3] * lv[3:4], axis=1, keepdims=True)) + lambda_init)
    n_grp = m_sc.shape[0]
    for first in range(0, DIFF_HEADS, n_grp):
        slabs = [slice(hd * LANES, (hd + 1) * LANES) for hd in range(first, first + n_grp)]
        qs = []
        for sl in slabs:
            qh = q_ref[0, :, sl]
            zero = jnp.zeros_like(qh)
            qs.append(jnp.concatenate(
                [jnp.where(comp0, qh, zero), jnp.where(comp0, zero, qh)], axis=0))
        v_ats = [lambda j, raw=_key_step(v_ref, sl): jnp.concatenate([raw(j), one], axis=1) for sl in slabs]
        _flash_group(qs, [_key_step(k_ref, sl) for sl in slabs], v_ats, n_full, row_pos,
                     m_sc, acc_sc, p_sc, alpha_sc)
        for g, sl in enumerate(slabs):
            o = acc_sc[g, :, :LANES] / acc_sc[g, :, LANES:]
            o = o[:tq] - lam * o[tq:]
            o_ref[0, :, sl] = (_rms(o, g_ref[...]) * (1.0 - lambda_init)).astype(BF16)
            acc_sc[g] = jnp.zeros(acc_sc.shape[1:], F32)


def _diff_attention(q, k, v, lam_vecs, subln_g, lambda_init):
    b, s, w = q.shape
    tq = DIFF_TQ
    return pl.pallas_call(
        functools.partial(_diff_attn_kernel, lambda_init=lambda_init),
        grid=(b, s // tq),
        in_specs=[
            pl.BlockSpec((1, tq, w), lambda i, j: (i, j, 0)),
            pl.BlockSpec((1, s, w), lambda i, j: (i, 0, 0)),
            pl.BlockSpec((1, s, w), lambda i, j: (i, 0, 0)),
            _const_spec(lam_vecs.shape), _const_spec(subln_g.shape),
        ],
        out_specs=pl.BlockSpec((1, tq, w), lambda i, j: (i, j, 0)),
        out_shape=jax.ShapeDtypeStruct((b, s, w), BF16),
        scratch_shapes=_flash_scratch(2 * tq, 2 * LANES),
        compiler_params=pltpu.CompilerParams(
            dimension_semantics=("arbitrary", "arbitrary"), vmem_limit_bytes=VMEM_LIMIT),
        name="diff_attention",
    )(q, k, v, lam_vecs, subln_g)


def _post_kernel(*refs, n_mix, final_norm):
    mix_refs = refs[:n_mix]
    x_ref, mod_ref, wout_ref, n2g_ref, wgu_ref, wd_ref = refs[n_mix:n_mix + 6]
    rest = refs[n_mix + 6:]
    if final_norm:
        fng_ref, o_ref, acc_sc = rest
    else:
        o_ref, acc_sc = rest
    mod = mod_ref[0]
    y = None
    row = 0
    for r in mix_refs:
        w = r.shape[2]
        part = jnp.dot(r[0], wout_ref[row:row + w, :], preferred_element_type=F32)
        y = part if y is None else y + part
        row += w
    x1 = x_ref[0] + mod[2:3] * y
    h = (_rms(x1, n2g_ref[...]) * (1.0 + mod[4:5]) + mod[3:4]).astype(BF16)

    for c in range(N_FF_CHUNKS):
        cols = slice(c * FF_CHUNK, (c + 1) * FF_CHUNK)
        ucols = slice(D_FF + c * FF_CHUNK, D_FF + (c + 1) * FF_CHUNK)
        g = jnp.dot(h, wgu_ref[0, :, cols], preferred_element_type=F32)
        u = jnp.dot(h, wgu_ref[0, :, ucols], preferred_element_type=F32)
        act = (g * jax.nn.sigmoid(g) * u).astype(BF16)
        down = jnp.dot(act, wd_ref[0, cols, :], preferred_element_type=F32)
        if c == 0:
            acc_sc[...] = down
        else:
            acc_sc[...] += down
    x2 = x1 + mod[5:6] * acc_sc[...]
    if final_norm:
        x2 = _rms(x2, fng_ref[...])
    o_ref[0] = x2


def _post(mix_inputs, x, mod, wout, n2g, wgu, wd, layer, final_g=None):
    b, s, d = x.shape
    t = POST_TILE
    tok = lambda w: pl.BlockSpec((1, t, w), lambda i, j: (i, j, 0))
    layer_spec = lambda a: pl.BlockSpec((1,) + a.shape[1:], lambda i, j: (layer, 0, 0),
                                        pipeline_mode=pl.Buffered(1))
    final_norm = final_g is not None
    in_specs = [tok(m.shape[2]) for m in mix_inputs] + [
        tok(d),
        pl.BlockSpec((1, 6, d), lambda i, j: (i, 0, 0)),
        _const_spec(wout.shape), _const_spec(n2g.shape), layer_spec(wgu), layer_spec(wd),
    ]
    args = list(mix_inputs) + [x, mod, wout, n2g, wgu, wd]
    if final_norm:
        in_specs.append(_const_spec(final_g.shape))
        args.append(final_g)
    return pl.pallas_call(
        functools.partial(_post_kernel, n_mix=len(mix_inputs), final_norm=final_norm),
        grid=(b, s // t),
        in_specs=in_specs,
        out_specs=tok(d),
        out_shape=jax.ShapeDtypeStruct((b, s, d), F32),
        scratch_shapes=[pltpu.VMEM((t, d), F32)],
        compiler_params=pltpu.CompilerParams(
            dimension_semantics=("arbitrary", "arbitrary"), vmem_limit_bytes=VMEM_LIMIT),
        name="post_final" if final_norm else "post",
    )(*args)


def _mla_slab(nope, rope):
    split = MLA_X1_LANE
    half = MLA_ROPE // 2
    pad = jnp.zeros(nope.shape[:-1] + (LANES - MLA_NOPE - MLA_ROPE,), nope.dtype)
    return jnp.concatenate([nope[..., :split], rope[..., :half], nope[..., split:], pad, rope[..., half:]], axis=-1)


def _diff_qkv_weights(w_qkv):
    half = ROT_DIFF // 2
    c0, c1 = 0, DIFF_HD
    order = (list(range(c0, c0 + half)) + list(range(c1, c1 + half)) + list(range(c0 + ROT_DIFF, c0 + DIFF_HD))
             + list(range(c0 + half, c0 + ROT_DIFF)) + list(range(c1 + half, c1 + ROT_DIFF))
             + list(range(c1 + ROT_DIFF, c1 + DIFF_HD)))
    assert sorted(order) == list(range(LANES))
    width = DIFF_HEADS * LANES
    perm = jnp.asarray([hd * LANES + o for hd in range(DIFF_HEADS) for o in order], jnp.int32)
    return jnp.concatenate([w_qkv[:, :width][:, perm], w_qkv[:, width:2 * width][:, perm], w_qkv[:, 2 * width:]],
                           axis=1).astype(BF16)


def kernel(x, c, positions, ada_w, ada_b, norm1_g, norm2_g, ffn_w_gate_up, ffn_w_down, mla_w_in, mla_q_norm_g,
           mla_kv_norm_g, mla_w_uq, mla_w_ukv, pool_w, pool_b, pool_scale, mix_a_w_out, diff_w_qkv,
           diff_lambda_q1, diff_lambda_k1, diff_lambda_q2, diff_lambda_k2, diff_subln_g, diff_w_out,
           final_norm_g):
    d = D_MODEL
    rope_tab, mod = _rope_and_modulation(positions, c, ada_w, ada_b)

    w_in = mla_w_in[0]
    o_kv, o_kr, o_u = Q_LORA, Q_LORA + KV_LORA, Q_LORA + KV_LORA + MLA_ROPE
    w_kr = _mla_slab(jnp.zeros((d, MLA_NOPE), F32), w_in[:, o_kr:o_u])
    w_in_ext = jnp.concatenate([w_in[:, :o_kr], w_kr, w_in[:, o_u:]], axis=1).astype(BF16)
    w_uq = mla_w_uq[0].reshape(Q_LORA, MLA_HEADS, MLA_NOPE + MLA_ROPE)
    wq = _mla_slab(w_uq[..., :MLA_NOPE], w_uq[..., MLA_NOPE:]).reshape(Q_LORA, MLA_HEADS * LANES).astype(BF16)
    w_ukv = mla_w_ukv[0].reshape(KV_LORA, MLA_HEADS, MLA_NOPE + MLA_V)
    wk = _mla_slab(w_ukv[..., :MLA_NOPE], jnp.zeros((KV_LORA, MLA_HEADS, MLA_ROPE), F32))
    wk = wk.reshape(KV_LORA, MLA_HEADS * LANES)
    wv = w_ukv[..., MLA_NOPE:].reshape(KV_LORA, MLA_HEADS * MLA_V)
    wkv = jnp.concatenate([wk, wv], axis=1).astype(BF16)
    q0, k0, v0, pool = _l0_pre(
        x, mod[0], norm1_g[0:1], w_in_ext, mla_q_norm_g, mla_kv_norm_g, wq, wkv, rope_tab,
        pool_w[0].astype(BF16), pool_b[0].reshape(1, POOL_WIDTH), pool_scale)
    attn0 = _mla_attention(q0, k0, v0)
    wgu, wd = ffn_w_gate_up.astype(BF16), ffn_w_down.astype(BF16)
    x = _post([attn0, pool], x, mod[0], mix_a_w_out[0].astype(BF16), norm2_g[0:1], wgu, wd, 0)

    lambda_init = 0.8 - 0.6 * math.exp(-0.3 * 1)
    q1, k1, v1 = _l1_pre(x, mod[1], norm1_g[1:2], _diff_qkv_weights(diff_w_qkv[0]), rope_tab)
    lam_vecs = jnp.concatenate([diff_lambda_q1, diff_lambda_k1, diff_lambda_q2, diff_lambda_k2], axis=0)
    attn1 = _diff_attention(q1, k1, v1, lam_vecs, diff_subln_g, lambda_init)
    return _post([attn1], x, mod[1], diff_w_out[0].astype(BF16), norm2_g[1:2], wgu, wd, 1,
                 final_g=final_norm_g.reshape(1, d))
```
